```python
import math
import jax
import jax.numpy as jnp
from jax import lax
import numpy as np

D_MODEL = 1024
BATCH = 16
SEQ = 2048
DEPTH = 4

HEAD_DIM = 64
N_GROUPS = 4
MIX_WIDTH = D_MODEL
GROUP_WIDTH = MIX_WIDTH // N_GROUPS
GROUP_HEADS = GROUP_WIDTH // HEAD_DIM
D_FF = 256 * ((8 * D_MODEL // 3 + 255) // 256)
PLE_DIM = 256
ROPE_THETA = 10000.0
EPS = 1e-6
NEG_INF = -1e30

GRID_W = 64
NA_HEADS = GROUP_HEADS
NA_ROWS = 8
NA_COLS = 16
SWA_HEADS = GROUP_HEADS
SWA_KV_HEADS = GROUP_HEADS // 2
SWA_GROUP = SWA_HEADS // SWA_KV_HEADS
SWA_HALF = 128
SWA_BLOCK = 128
DIL_HEADS = GROUP_HEADS
DIL_PATTERNS = ((128, 1), (512, 4), (2048, 16))
DIL_BLOCK = 64
MLA_HEADS = GROUP_HEADS
MLA_Q_LORA = D_MODEL // 4
MLA_KV_LORA = D_MODEL // 8
MLA_NOPE = HEAD_DIM
MLA_ROPE = HEAD_DIM // 2
MLA_V = GROUP_WIDTH // MLA_HEADS
DENSE_BLOCK = 128

NA_IN = 3 * NA_HEADS * HEAD_DIM
SWA_IN = (SWA_HEADS + 2 * SWA_KV_HEADS) * HEAD_DIM
DIL_IN = 3 * DIL_HEADS * HEAD_DIM
MLA_IN = MLA_Q_LORA + MLA_KV_LORA + MLA_ROPE
IN_COLS = NA_IN + SWA_IN + DIL_IN + MLA_IN
IN_SPLITS = (NA_IN, NA_IN + SWA_IN, NA_IN + SWA_IN + DIL_IN)

kernel_name = "hybrid_parallel_headgroup_encoder"


def rms_norm(x, g):
    xf = x.astype(jnp.float32)
    y = xf * lax.rsqrt(jnp.mean(xf * xf, axis=-1, keepdims=True) + EPS)
    return (y * g.astype(jnp.float32)).astype(x.dtype)


def rope(x, pos):
    half = x.shape[-1] // 2
    inv_freq = ROPE_THETA ** (-jnp.arange(half, dtype=jnp.float32) / half)
    ang = pos.astype(jnp.float32)[:, None] * inv_freq[None, :]
    cos, sin = jnp.cos(ang), jnp.sin(ang)
    xf = x.astype(jnp.float32)
    x1, x2 = xf[..., :half], xf[..., half:]
    return jnp.concatenate([x1 * cos - x2 * sin, x2 * cos + x1 * sin], axis=-1).astype(x.dtype)


def swiglu(u, w_gate, w_up, w_down):
    return (jax.nn.silu(u @ w_gate) * (u @ w_up)) @ w_down


def split_heads(t, n):
    b, s, _ = t.shape
    return t.reshape(b, s, n, -1).transpose(0, 2, 1, 3)


def merge_heads(t):
    b, h, s, d = t.shape
    return t.transpose(0, 2, 1, 3).reshape(b, s, h * d)


def band_attention_stats(q, k, v, half_width, block, scale):
    L = q.shape[-2]
    qb_len = math.gcd(L, block)
    nb = L // qb_len
    kw = qb_len + 2 * half_width
    pad = [(0, 0)] * (k.ndim - 2) + [(half_width, half_width), (0, 0)]
    kp = jnp.pad(k, pad)
    vp = jnp.pad(v, pad)
    starts = jnp.arange(nb) * qb_len
    idx = starts[:, None] + jnp.arange(kw)[None, :]
    kb = jnp.take(kp, idx, axis=-2)
    vb = jnp.take(vp, idx, axis=-2)
    qb = q.reshape(q.shape[:-2] + (nb, qb_len, q.shape[-1]))
    s = jnp.einsum("...gnqd,...nkd->...gnqk", qb, kb).astype(jnp.float32) * scale
    qpos = starts[:, None, None] + jnp.arange(qb_len)[None, :, None]
    kpos = starts[:, None, None] - half_width + jnp.arange(kw)[None, None, :]
    valid = (jnp.abs(kpos - qpos) <= half_width) & (kpos >= 0) & (kpos < L)
    s = jnp.where(valid, s, NEG_INF)
    m = jnp.max(s, axis=-1)
    p = jnp.exp(s - m[..., None])
    l = jnp.sum(p, axis=-1)
    o = jnp.einsum("...gnqk,...nkd->...gnqd", p, vb)
    lead = q.shape[:-2]
    return (m.reshape(lead + (L,)), l.reshape(lead + (L,)), o.reshape(lead + (L, o.shape[-1])))


def neighbourhood_attention(q, k, v, bias_table):
    b, h, t, dh = q.shape
    rows = t // GRID_W
    kr = min(NA_ROWS, rows)
    n_cb = GRID_W // NA_COLS
    kc = 2 * NA_COLS
    r = jnp.arange(rows)
    key_row = jnp.clip(r - kr // 2, 0, rows - kr)[:, None] + jnp.arange(kr)[None, :]
    c0 = jnp.arange(n_cb) * NA_COLS
    key_col = jnp.clip(c0 - NA_COLS // 2, 0, GRID_W - kc)[:, None] + jnp.arange(kc)[None, :]
    key_idx = (key_row[:, None, :, None] * GRID_W + key_col[None, :, None, :]).reshape(rows, n_cb, kr * kc)
    kb = jnp.take(k, key_idx, axis=2)
    vb = jnp.take(v, key_idx, axis=2)
    qb = q.reshape(b, h, rows, n_cb, NA_COLS, dh)
    s = jnp.einsum("bhrcqd,bhrckd->bhrcqk", qb, kb).astype(jnp.float32) * dh ** -0.5
    q_col = c0[:, None] + jnp.arange(NA_COLS)[None, :]
    win_start = jnp.clip(q_col - NA_COLS // 2, 0, GRID_W - NA_COLS)
    kcol = key_col[:, None, None, :]
    ws = win_start[:, :, None, None]
    valid = (kcol >= ws) & (kcol < ws + NA_COLS)
    valid = jnp.broadcast_to(valid, (n_cb, NA_COLS, kr, kc)).reshape(n_cb, NA_COLS, kr * kc)
    d_row = (key_row - r[:, None] + NA_ROWS - 1)[:, None, None, :, None]
    d_col = (jnp.clip(kcol - q_col[:, :, None, None], 1 - NA_COLS, NA_COLS - 1) + NA_COLS - 1)[None]
    bias = bias_table[:, d_row, d_col].reshape(h, rows, n_cb, NA_COLS, kr * kc)
    s = jnp.where(valid, s + bias.astype(jnp.float32), NEG_INF)
    p = jax.nn.softmax(s, axis=-1)
    o = jnp.einsum("bhrcqk,bhrckd->bhrcqd", p.astype(v.dtype), vb)
    return o.reshape(b, h, t, dh)


def na_mixer(z, bias_table):
    q, k, v = jnp.split(z, 3, axis=-1)
    o = neighbourhood_attention(split_heads(q, NA_HEADS), split_heads(k, NA_HEADS),
                                split_heads(v, NA_HEADS), bias_table)
    return merge_heads(o)


def swa_mixer(z, sink, pos):
    b, t, _ = z.shape
    q, k, v = jnp.split(z, [SWA_HEADS * HEAD_DIM, (SWA_HEADS + SWA_KV_HEADS) * HEAD_DIM], axis=-1)
    q = rope(split_heads(q, SWA_HEADS), pos).reshape(b, SWA_KV_HEADS, SWA_GROUP, t, HEAD_DIM)
    k = rope(split_heads(k, SWA_KV_HEADS), pos)
    v = split_heads(v, SWA_KV_HEADS)
    m, l, o = band_attention_stats(q, k, v, SWA_HALF, SWA_BLOCK, HEAD_DIM ** -0.5)
    sk = sink.reshape(SWA_KV_HEADS, SWA_GROUP)[:, :, None].astype(jnp.float32)
    m2 = jnp.maximum(m, sk)
    a = jnp.exp(m - m2)
    o = o * (a / (l * a + jnp.exp(sk - m2)))[..., None]
    return merge_heads(o.reshape(b, SWA_HEADS, t, HEAD_DIM).astype(z.dtype))


def to_residue(x, dil):
    t, d = x.shape[-2], x.shape[-1]
    return jnp.swapaxes(x.reshape(x.shape[:-2] + (t // dil, dil, d)), -3, -2)


def from_residue(x):
    y = jnp.swapaxes(x, -3, -2)
    return y.reshape(y.shape[:-3] + (y.shape[-3] * y.shape[-2], y.shape[-1]))


def dil_mixer(z, pos):
    b, t, _ = z.shape
    q, k, v = jnp.split(z, 3, axis=-1)
    q = rope(split_heads(q, DIL_HEADS), pos)
    k = rope(split_heads(k, DIL_HEADS), pos)
    v = split_heads(v, DIL_HEADS)
    ms, ls, os_ = [], [], []
    for window, dil in DIL_PATTERNS:
        steps = window // 2 // dil
        m, l, o = band_attention_stats(to_residue(q, dil)[..., None, :, :], to_residue(k, dil),
                                       to_residue(v, dil), steps, DIL_BLOCK, HEAD_DIM ** -0.5)
        ms.append(from_residue(m[..., 0, :, None])[..., 0])
        ls.append(from_residue(l[..., 0, :, None])[..., 0])
        os_.append(from_residue(o[..., 0, :, :]))
    m_all = jnp.stack(ms)
    w = jnp.exp(m_all - jnp.max(m_all, axis=0))
    den = jnp.sum(jnp.stack(ls) * w, axis=0)
    num = jnp.sum(jnp.stack(os_) * w[..., None], axis=0)
    return merge_heads((num / den[..., None]).astype(z.dtype))


def dense_block_attention(q, k, v, scale):
    b, h, t, dk = q.shape
    qb_len = math.gcd(t, DENSE_BLOCK)
    qb = q.reshape(b, h, t // qb_len, qb_len, dk).transpose(2, 0, 1, 3, 4)

    def one_block(q_blk):
        s = jnp.einsum("bhqd,bhkd->bhqk", q_blk, k).astype(jnp.float32) * scale
        p = jax.nn.softmax(s, axis=-1)
        return jnp.einsum("bhqk,bhkd->bhqd", p.astype(v.dtype), v)

    o = lax.map(one_block, qb)
    return o.transpose(1, 2, 0, 3, 4).reshape(b, h, t, v.shape[-1])


def mla_mixer(z, q_norm, w_uq, kv_norm, w_ukv, pos):
    b, t, _ = z.shape
    cq, ckv, kpe = jnp.split(z, [MLA_Q_LORA, MLA_Q_LORA + MLA_KV_LORA], axis=-1)
    qf = (rms_norm(cq, q_norm) @ w_uq).reshape(b, t, MLA_HEADS, MLA_NOPE + MLA_ROPE).transpose(0, 2, 1, 3)
    q = jnp.concatenate([qf[..., :MLA_NOPE], rope(qf[..., MLA_NOPE:], pos)], axis=-1)
    kv = (rms_norm(ckv, kv_norm) @ w_ukv).reshape(b, t, MLA_HEADS, MLA_NOPE + MLA_V).transpose(0, 2, 1, 3)
    k_pe = rope(kpe, pos)[:, None]
    k = jnp.concatenate([kv[..., :MLA_NOPE], jnp.broadcast_to(k_pe, (b, MLA_HEADS, t, MLA_ROPE))], axis=-1)
    v = kv[..., MLA_NOPE:]
    o = dense_block_attention(q, k, v, (MLA_NOPE + MLA_ROPE) ** -0.5)
    return merge_heads(o)


def setup_inputs(seed: int = 0) -> dict:
    key = jax.random.key(seed)
    ks = jax.random.split(key, 26)
    f32 = jnp.float32

    def w(k, shape, fan_in):
        return jax.random.normal(k, shape, f32) * (fan_in ** -0.5)

    def g(k, shape):
        return 1.0 + 0.02 * jax.random.normal(k, shape, f32)

    L = DEPTH
    return {
        "x": jax.random.normal(ks[0], (BATCH, SEQ, D_MODEL), f32),
        "p": jax.random.normal(ks[1], (DEPTH, BATCH, SEQ, PLE_DIM), f32),
        "ffn1_norm": g(ks[2], (L, D_MODEL)),
        "ffn1_w_gate": w(ks[3], (L, D_MODEL, D_FF), D_MODEL),
        "ffn1_w_up": w(ks[4], (L, D_MODEL, D_FF), D_MODEL),
        "ffn1_w_down": w(ks[5], (L, D_FF, D_MODEL), D_FF),
        "mix_norm": g(ks[6], (L, D_MODEL)),
        "w_in": w(ks[7], (L, D_MODEL, IN_COLS), D_MODEL),
        "na_bias": 0.1 * jax.random.normal(ks[8], (L, NA_HEADS, 2 * NA_ROWS - 1, 2 * NA_COLS - 1), f32),
        "swa_sink": 0.5 * jax.random.normal(ks[9], (L, SWA_HEADS), f32),
        "mla_q_norm": g(ks[10], (L, MLA_Q_LORA)),
        "mla_w_uq": w(ks[11], (L, MLA_Q_LORA, MLA_HEADS * (MLA_NOPE + MLA_ROPE)), MLA_Q_LORA),
        "mla_kv_norm": g(ks[12], (L, MLA_KV_LORA)),
        "mla_w_ukv": w(ks[13], (L, MLA_KV_LORA, MLA_HEADS * (MLA_NOPE + MLA_V)), MLA_KV_LORA),
        "group_norm": g(ks[14], (L, N_GROUPS, GROUP_WIDTH)),
        "w_out": w(ks[15], (L, MIX_WIDTH, D_MODEL), MIX_WIDTH),
        "ffn2_norm": g(ks[16], (L, D_MODEL)),
        "ffn2_w_gate": w(ks[17], (L, D_MODEL, D_FF), D_MODEL),
        "ffn2_w_up": w(ks[18], (L, D_MODEL, D_FF), D_MODEL),
        "ffn2_w_down": w(ks[19], (L, D_FF, D_MODEL), D_FF),
        "ple_norm": g(ks[20], (L, D_MODEL)),
        "ple_w_gate": w(ks[21], (L, D_MODEL, D_MODEL), D_MODEL),
        "ple_w_proj": w(ks[22], (L, PLE_DIM, D_MODEL), PLE_DIM),
        "final_norm": g(ks[23], (D_MODEL,)),
    }


def reference(x, p, ffn1_norm, ffn1_w_gate, ffn1_w_up, ffn1_w_down, mix_norm, w_in, na_bias,
              swa_sink, mla_q_norm, mla_w_uq, mla_kv_norm, mla_w_ukv, group_norm, w_out,
              ffn2_norm, ffn2_w_gate, ffn2_w_up, ffn2_w_down, ple_norm, ple_w_gate, ple_w_proj,
              final_norm):
    t = x.shape[1]
    pos = jnp.arange(t, dtype=jnp.int32)
    for i in range(DEPTH):
        x = x + 0.5 * swiglu(rms_norm(x, ffn1_norm[i]), ffn1_w_gate[i], ffn1_w_up[i], ffn1_w_down[i])
        z = rms_norm(x, mix_norm[i]) @ w_in[i]
        z_na, z_swa, z_dil, z_mla = jnp.split(z, IN_SPLITS, axis=-1)
        y_na = na_mixer(z_na, na_bias[i])
        y_swa = swa_mixer(z_swa, swa_sink[i], pos)
        y_dil = dil_mixer(z_dil, pos)
        y_mla = mla_mixer(z_mla, mla_q_norm[i], mla_w_uq[i], mla_kv_norm[i], mla_w_ukv[i], pos)
        y = jnp.concatenate([rms_norm(y_na, group_norm[i, 0]), rms_norm(y_swa, group_norm[i, 1]),
                             rms_norm(y_dil, group_norm[i, 2]), rms_norm(y_mla, group_norm[i, 3])], axis=-1)
        x = x + y @ w_out[i]
        x = x + 0.5 * swiglu(rms_norm(x, ffn2_norm[i]), ffn2_w_gate[i], ffn2_w_up[i], ffn2_w_down[i])
        gate = jax.nn.sigmoid(rms_norm(x, ple_norm[i]) @ ple_w_gate[i])
        x = x + gate * (p[i] @ ple_w_proj[i])
    return rms_norm(x, final_norm)
```

```python
import functools
import math

import numpy as np
import jax
import jax.numpy as jnp
from jax import lax
from jax.experimental import pallas as pl
from jax.experimental.pallas import tpu as pltpu

F32 = jnp.float32
BF16 = jnp.bfloat16

LANES = 128
VMEM_LIMIT_BYTES = 58 * 1024 * 1024

D_MODEL = 1024
SEQ = 2048
DEPTH = 4
HEAD_DIM = 64
GROUP_WIDTH = 256
D_FF = 2816
FF_CHUNK = 256
N_FF_CHUNKS = D_FF // FF_CHUNK
PLE_DIM = 256
ROPE_THETA = 10000.0
EPS = 1e-6
NEG_INF = -1e30

GRID_W = 64
NA_ROWS = 8
NA_COLS = 16
NA_KEYS = NA_ROWS * GRID_W
SWA_HALF = 128
SWA_BLOCK = 128
SWA_KEYS = SWA_BLOCK + 2 * SWA_HALF
DIL_PATTERNS = ((128, 1), (512, 4), (2048, 16))
DIL_STEPS = 64
DIL_BLOCK = 128
DIL_KEYS = DIL_BLOCK + 2 * DIL_STEPS
MLA_Q_LORA = 256
MLA_KV_LORA = 128
MLA_NOPE = 64
MLA_ROPE = 32
MLA_QBLOCK = 256

NA_OFF, SWA_OFF, DIL_OFF, MLA_OFF = 0, 768, 1280, 2048
P_NA, P_SWA, P_DIL, P_MLA = 0, 768, 1536, 2304
P_COLS = 2944
MLA_COLS = 640

TOKEN_TILE = 512


def _cparams(n_axes=1):
    return pltpu.CompilerParams(dimension_semantics=("arbitrary",) * n_axes,
                                vmem_limit_bytes=VMEM_LIMIT_BYTES)


def _resident(shape):
    nd = len(shape)
    return pl.BlockSpec(shape, lambda *_: (0,) * nd, pipeline_mode=pl.Buffered(1))


def _rms(x, g):
    return x * lax.rsqrt(jnp.mean(x * x, axis=-1, keepdims=True) + EPS) * g


def _dot(a, b):
    return jnp.dot(a, b, preferred_element_type=F32)


def _dot_t(a, b):
    return lax.dot_general(a, b, (((1,), (1,)), ((), ())), preferred_element_type=F32)


def _swiglu_half_step(x, norm_ref, wg_ref, wu_ref, wd_ref, u_ref, acc_ref):
    u_ref[...] = _rms(x, norm_ref[...]).astype(BF16)
    acc_ref[...] = jnp.zeros_like(acc_ref)

    def chunk(j, carry):
        u = u_ref[...]
        g = _dot(u, wg_ref[j])
        up = _dot(u, wu_ref[j])
        a = (g * jax.nn.sigmoid(g) * up).astype(BF16)
        acc_ref[...] += _dot(a, wd_ref[j])
        return carry

    lax.fori_loop(0, N_FF_CHUNKS, chunk, 0)
    return x + 0.5 * acc_ref[...]


def _rope_tile(x, cos, sin_signed):
    lane = lax.broadcasted_iota(jnp.int32, x.shape, 1)
    first_half = (lane % HEAD_DIM) < (HEAD_DIM // 2)
    partner = jnp.where(first_half, pltpu.roll(x, LANES - HEAD_DIM // 2, 1), pltpu.roll(x, HEAD_DIM // 2, 1))
    return x * cos + partner * sin_signed


def _ffn1_proj_kernel(x_ref, n1_ref, wg_ref, wu_ref, wd_ref, mixn_ref, win_ref, cos_ref, sin_ref,
                      xo_ref, zna_ref, zswa_ref, zdil_ref, zmla_ref, u_ref, acc_ref):
    x = _swiglu_half_step(x_ref[...], n1_ref, wg_ref, wu_ref, wd_ref, u_ref, acc_ref)
    xo_ref[...] = x
    u_ref[...] = _rms(x, mixn_ref[...]).astype(BF16)
    cos = cos_ref[...]
    sin = sin_ref[...]
    q_scale = HEAD_DIM ** -0.5

    def proj(col, width):
        return _dot(u_ref[...], win_ref[:, col:col + width])

    def roped(col, scale):
        z = proj(col, 2 * LANES)
        halves = [_rope_tile(z[:, h * LANES:(h + 1) * LANES], cos, sin) for h in range(2)]
        return jnp.concatenate(halves, axis=1) * scale

    zna_ref[:, 0:256] = (proj(P_NA, 256) * q_scale).astype(BF16)
    zna_ref[:, 256:768] = proj(P_NA + 256, 512).astype(BF16)
    zswa_ref[:, 0:256] = roped(P_SWA, q_scale).astype(BF16)
    zswa_ref[:, 256:512] = roped(P_SWA + 256, 1.0).astype(BF16)
    zswa_ref[:, 512:768] = proj(P_SWA + 512, 256).astype(BF16)
    zdil_ref[:, 0:256] = roped(P_DIL, q_scale)
    zdil_ref[:, 256:512] = roped(P_DIL + 256, 1.0)
    zdil_ref[:, 512:768] = proj(P_DIL + 512, 256)
    zmla_ref[:, 0:384] = proj(P_MLA, 384)
    zmla_ref[:, 384:640] = proj(P_MLA + 384, 256)


def _out_ffn2_ple_kernel(final, x_ref, yna_ref, yswa_ref, ydil_ref, ymla_ref, p_ref, wout_ref,
                         n2_ref, wg_ref, wu_ref, wd_ref, plen_ref, plewg_ref, plewp_ref, fin_ref,
                         xo_ref, u_ref, acc_ref):
    x = x_ref[...]
    mix = _dot(yna_ref[...], wout_ref[0:256, :])
    mix += _dot(yswa_ref[...], wout_ref[256:512, :])
    mix += _dot(ydil_ref[...], wout_ref[512:768, :])
    mix += _dot(ymla_ref[...], wout_ref[768:1024, :])
    x = x + mix
    x = _swiglu_half_step(x, n2_ref, wg_ref, wu_ref, wd_ref, u_ref, acc_ref)
    gate = jax.nn.sigmoid(_dot(_rms(x, plen_ref[...]).astype(BF16), plewg_ref[...]))
    x = x + gate * _dot(p_ref[...].astype(BF16), plewp_ref[...])
    if final:
        x = _rms(x, fin_ref[...])
    xo_ref[...] = x


def _pair_attention(q, k, v, bias2, sink2=None):
    lane = lax.broadcasted_iota(jnp.int32, q.shape, 1)
    zero = jnp.zeros_like(q)
    qq = jnp.concatenate([jnp.where(lane < HEAD_DIM, q, zero), jnp.where(lane >= HEAD_DIM, q, zero)], axis=0)
    s = _dot_t(qq, k) + bias2
    m = jnp.max(s, axis=-1, keepdims=True)
    if sink2 is not None:
        m = jnp.maximum(m, sink2)
    p = jnp.exp(s - m)
    l = jnp.sum(p, axis=-1, keepdims=True)
    if sink2 is not None:
        l = l + jnp.exp(sink2 - m)
    o = _dot(p.astype(BF16), v)
    return m, l, o


def _merge_heads(x2, rows):
    top, bot = x2[:rows], x2[rows:]
    if top.shape[1] == 1:
        top = jnp.broadcast_to(top, (rows, LANES))
        bot = jnp.broadcast_to(bot, (rows, LANES))
    lane = lax.broadcasted_iota(jnp.int32, (rows, LANES), 1)
    return jnp.where(lane < HEAD_DIM, top, bot)


def _band_bias(q_start, k_start, n_q, n_k, half_width):
    qpos = q_start + lax.broadcasted_iota(jnp.int32, (n_q, n_k), 0)
    kpos = k_start + lax.broadcasted_iota(jnp.int32, (n_q, n_k), 1)
    bias = jnp.where(jnp.abs(kpos - qpos) <= half_width, 0.0, NEG_INF).astype(F32)
    return jnp.concatenate([bias, bias], axis=0)


def _na_kernel(z_ref, bias_ref, gn_ref, o_ref):
    rows = SEQ // GRID_W

    def row(r, carry):
        rs = jnp.clip(r - NA_ROWS // 2, 0, rows - NA_ROWS)
        shift = rs - r + NA_ROWS - 1
        q0 = pl.multiple_of(r * GRID_W, GRID_W)
        k0 = pl.multiple_of(rs * GRID_W, GRID_W)
        outs = []
        for pr in range(2):
            q = z_ref[0, pl.ds(q0, GRID_W), pr * LANES:(pr + 1) * LANES]
            k = z_ref[0, pl.ds(k0, NA_KEYS), 256 + pr * LANES:256 + (pr + 1) * LANES]
            v = z_ref[0, pl.ds(k0, NA_KEYS), 512 + pr * LANES:512 + (pr + 1) * LANES]
            _, l, o = _pair_attention(q, k, v, bias_ref[pr, shift])
            outs.append(_merge_heads(o / l, GRID_W))
        y = jnp.concatenate(outs, axis=1)
        o_ref[0, pl.ds(q0, GRID_W), :] = _rms(y, gn_ref[...]).astype(BF16)
        return carry

    lax.fori_loop(0, rows, row, 0)


def _swa_kernel(sink_ref, z_ref, gn_ref, o_ref):
    def block(i, carry):
        s0 = pl.multiple_of(i * SWA_BLOCK, SWA_BLOCK)
        ks = pl.multiple_of(jnp.clip(s0 - SWA_HALF, 0, SEQ - SWA_KEYS), SWA_BLOCK)
        bias2 = _band_bias(s0, ks, SWA_BLOCK, SWA_KEYS, SWA_HALF)
        head_row = lax.broadcasted_iota(jnp.int32, (2 * SWA_BLOCK, 1), 0) < SWA_BLOCK
        outs = []
        for g in range(2):
            q = z_ref[0, pl.ds(s0, SWA_BLOCK), g * LANES:(g + 1) * LANES]
            k = z_ref[0, pl.ds(ks, SWA_KEYS), 256 + g * LANES:256 + (g + 1) * LANES]
            v = z_ref[0, pl.ds(ks, SWA_KEYS), 512 + g * LANES:512 + (g + 1) * LANES]
            sink2 = jnp.where(head_row, sink_ref[2 * g], sink_ref[2 * g + 1]).astype(F32)
            _, l, o = _pair_attention(q, k, v, bias2, sink2)
            outs.append(_merge_heads(o / l, SWA_BLOCK))
        y = jnp.concatenate(outs, axis=1)
        o_ref[0, pl.ds(s0, SWA_BLOCK), :] = _rms(y, gn_ref[...]).astype(BF16)
        return carry

    lax.fori_loop(0, SEQ // SWA_BLOCK, block, 0)


def _dil_kernel(q0_ref, q1_ref, k0_ref, k1_ref, v0_ref, v1_ref, gn_ref, o_ref, m_ref, l_ref, acc_ref):
    qkv_refs = ((q0_ref, k0_ref, v0_ref), (q1_ref, k1_ref, v1_ref))

    def attend(q_rows, k_rows, bias2, first):
        for pr in range(2):
            q_ref, k_ref, v_ref = qkv_refs[pr]
            q = q_ref[0, q_rows, :].astype(BF16)
            k = k_ref[0, k_rows, :].astype(BF16)
            v = v_ref[0, k_rows, :].astype(BF16)
            m, l, o = _pair_attention(q, k, v, bias2)
            m = _merge_heads(m, DIL_BLOCK)
            l = _merge_heads(l, DIL_BLOCK)
            o = _merge_heads(o, DIL_BLOCK)
            if first:
                m_ref[pr, q_rows, :] = m
                l_ref[pr, q_rows, :] = l
                acc_ref[pr, q_rows, :] = o
            else:
                m_old = m_ref[pr, q_rows, :]
                m_new = jnp.maximum(m_old, m)
                w_old = jnp.exp(m_old - m_new)
                w_blk = jnp.exp(m - m_new)
                m_ref[pr, q_rows, :] = m_new
                l_ref[pr, q_rows, :] = l_ref[pr, q_rows, :] * w_old + l * w_blk
                acc_ref[pr, q_rows, :] = acc_ref[pr, q_rows, :] * w_old + o * w_blk

    for p_idx, (window, dil) in enumerate(DIL_PATTERNS):
        res_len = SEQ // dil
        blocks_per_res = res_len // DIL_BLOCK
        n_keys = min(DIL_KEYS, res_len)
        steps = window // 2 // dil

        def step(idx, carry, dil=dil, blocks_per_res=blocks_per_res, n_keys=n_keys, res_len=res_len,
                 steps=steps, first=(p_idx == 0)):
            res = idx // blocks_per_res
            blk = idx % blocks_per_res
            q_lo = blk * DIL_BLOCK
            k_lo = jnp.clip(q_lo - steps, 0, res_len - n_keys)
            bias2 = _band_bias(q_lo, k_lo, DIL_BLOCK, n_keys, steps)
            if dil == 1:
                q_rows = pl.ds(pl.multiple_of(q_lo, DIL_BLOCK), DIL_BLOCK)
                k_rows = pl.ds(pl.multiple_of(k_lo, DIL_STEPS), n_keys)
            else:
                q_rows = pl.ds(res + dil * q_lo, DIL_BLOCK, stride=dil)
                k_rows = pl.ds(res + dil * k_lo, n_keys, stride=dil)
            attend(q_rows, k_rows, bias2, first)
            return carry

        lax.fori_loop(0, dil * blocks_per_res, step, 0)

    y = jnp.concatenate([acc_ref[0] / l_ref[0], acc_ref[1] / l_ref[1]], axis=1)
    o_ref[0] = _rms(y, gn_ref[...]).astype(BF16)


def _mla_kernel(z_ref, qn_ref, wuq_ref, kvn_ref, wukv_ref, c_ref, s_ref, gn_ref, o_ref,
                q_scr, k_scr, v_scr):
    scale = (MLA_NOPE + MLA_ROPE) ** -0.5
    c_tab = c_ref[...]
    s_tab = s_ref[...]
    cqn = _rms(z_ref[0, :, 0:MLA_Q_LORA], qn_ref[...]).astype(BF16)
    for h in range(4):
        g = _dot(cqn, wuq_ref[:, h * 256:(h + 1) * 256])
        q_scr[h] = ((g[:, :LANES] * c_tab + g[:, LANES:] * s_tab) * scale).astype(BF16)
    ckvn = _rms(z_ref[0, :, MLA_Q_LORA:MLA_Q_LORA + MLA_KV_LORA], kvn_ref[...]).astype(BF16)
    k_pe = z_ref[0, :, 384:512] * c_tab + z_ref[0, :, 512:640] * s_tab
    for h in range(4):
        k_scr[h] = (_dot(ckvn, wukv_ref[:, h * LANES:(h + 1) * LANES]) + k_pe).astype(BF16)
    v_all = _dot(ckvn, wukv_ref[:, 512:768])
    lane = lax.broadcasted_iota(jnp.int32, v_all.shape, 1)
    for h in range(4):
        v_scr[h] = jnp.where((lane // HEAD_DIM) == h, v_all, 0.0).astype(BF16)

    def qblock(i, carry):
        q0 = pl.multiple_of(i * MLA_QBLOCK, MLA_QBLOCK)
        y = jnp.zeros((MLA_QBLOCK, GROUP_WIDTH), F32)
        for h in range(4):
            s = _dot_t(q_scr[h, pl.ds(q0, MLA_QBLOCK), :], k_scr[h])
            m = jnp.max(s, axis=-1, keepdims=True)
            p = jnp.exp(s - m)
            l = jnp.sum(p, axis=-1, keepdims=True)
            y = y + _dot(p.astype(BF16), v_scr[h]) / l
        o_ref[0, pl.ds(q0, MLA_QBLOCK), :] = _rms(y, gn_ref[...]).astype(BF16)
        return carry

    lax.fori_loop(0, SEQ // MLA_QBLOCK, qblock, 0)


def _rope_tables_64():
    half = HEAD_DIM // 2
    inv_freq = np.float32(ROPE_THETA) ** (-np.arange(half, dtype=np.float32) / np.float32(half))
    ang = np.arange(SEQ, dtype=np.float32)[:, None] * inv_freq[None, :]
    cos, sin = np.cos(ang).astype(np.float32), np.sin(ang).astype(np.float32)
    cos_h = np.concatenate([cos, cos], axis=1)
    sin_h = np.concatenate([-sin, sin], axis=1)
    return jnp.asarray(np.tile(cos_h, (1, 2))), jnp.asarray(np.tile(sin_h, (1, 2)))


def _rope_tables_mla():
    half = MLA_ROPE // 2
    inv_freq = np.float32(ROPE_THETA) ** (-np.arange(half, dtype=np.float32) / np.float32(half))
    ang = np.arange(SEQ, dtype=np.float32)[:, None] * inv_freq[None, :]
    cos, sin = np.cos(ang).astype(np.float32), np.sin(ang).astype(np.float32)
    ones = np.ones((SEQ, MLA_NOPE), np.float32)
    zeros_n = np.zeros((SEQ, MLA_NOPE), np.float32)
    zeros_p = np.zeros((SEQ, LANES - MLA_NOPE - MLA_ROPE), np.float32)
    c_tab = np.concatenate([ones, cos, cos, zeros_p], axis=1)
    s_tab = np.concatenate([zeros_n, -sin, sin, zeros_p], axis=1)
    return jnp.asarray(c_tab), jnp.asarray(s_tab)


def _gather_cols(w, idx):
    idx = np.asarray(idx, np.int32)
    cols = jnp.take(w, jnp.asarray(np.maximum(idx, 0)), axis=1)
    return jnp.where(jnp.asarray(idx >= 0)[None, :], cols, 0.0)


def _swap_halves(n):
    return np.concatenate([np.arange(n // 2, n), np.arange(0, n // 2)])


def _proj_col_index():
    ar = np.arange
    zeros = lambda n: -np.ones(n, np.int64)
    na = NA_OFF + ar(768)
    swa_q = SWA_OFF + ar(256)
    k0, k1 = SWA_OFF + 256 + ar(64), SWA_OFF + 320 + ar(64)
    v0, v1 = SWA_OFF + 384 + ar(64), SWA_OFF + 448 + ar(64)
    swa = np.concatenate([swa_q, k0, k0, k1, k1, v0, v0, v1, v1])
    dil = DIL_OFF + ar(768)
    kpe = MLA_OFF + MLA_Q_LORA + MLA_KV_LORA + ar(MLA_ROPE)
    pad = LANES - MLA_NOPE - MLA_ROPE
    mla = np.concatenate([MLA_OFF + ar(MLA_Q_LORA + MLA_KV_LORA),
                          zeros(MLA_NOPE), kpe, zeros(pad),
                          zeros(MLA_NOPE), kpe[_swap_halves(MLA_ROPE)], zeros(pad)])
    idx = np.concatenate([na, swa, dil, mla])
    assert idx.shape[0] == P_COLS
    return idx


def _wuq_col_index():
    parts = []
    pad = LANES - MLA_NOPE - MLA_ROPE
    for h in range(4):
        base = h * (MLA_NOPE + MLA_ROPE)
        pe = base + MLA_NOPE + np.arange(MLA_ROPE)
        parts += [base + np.arange(MLA_NOPE), pe, -np.ones(pad, np.int64),
                  -np.ones(MLA_NOPE, np.int64), pe[_swap_halves(MLA_ROPE)], -np.ones(pad, np.int64)]
    return np.concatenate(parts)


def _wukv_col_index():
    parts = []
    for h in range(4):
        parts += [h * 128 + np.arange(MLA_NOPE), -np.ones(LANES - MLA_NOPE, np.int64)]
    for h in range(4):
        parts.append(h * 128 + MLA_NOPE + np.arange(HEAD_DIM))
    return np.concatenate(parts)


def _na_bias_tables(bias):
    qc = np.arange(GRID_W)[:, None]
    kc = np.arange(GRID_W)[None, :]
    ws = np.clip(qc - NA_COLS // 2, 0, GRID_W - NA_COLS)
    valid = (kc >= ws) & (kc < ws + NA_COLS)
    d_col = np.clip(kc - qc, 1 - NA_COLS, NA_COLS - 1) + NA_COLS - 1
    shift = np.arange(NA_ROWS)[:, None]
    d_row = shift + np.arange(NA_ROWS)[None, :]
    g = bias[:, jnp.asarray(d_row)[:, :, None, None], jnp.asarray(d_col)[None, None, :, :]]
    g = jnp.where(jnp.asarray(valid)[None, None, None], g, NEG_INF)
    g = g.transpose(0, 1, 3, 2, 4).reshape(4, NA_ROWS, GRID_W, NA_KEYS)
    return g.reshape(2, 2, NA_ROWS, GRID_W, NA_KEYS).transpose(0, 2, 1, 3, 4).reshape(2, NA_ROWS, 2 * GRID_W, NA_KEYS)


def _chunk_cols(w):
    return w.reshape(w.shape[0], N_FF_CHUNKS, FF_CHUNK).transpose(1, 0, 2).astype(BF16)


def _chunk_rows(w):
    return w.reshape(N_FF_CHUNKS, FF_CHUNK, w.shape[1]).astype(BF16)


def _tok_spec(width):
    return pl.BlockSpec((TOKEN_TILE, width), lambda i: (i, 0))


def _ffn1_proj(x2, n1, wg, wu, wd, mixn, win, cos, sin):
    n_tok = x2.shape[0]
    tiles_per_seq = SEQ // TOKEN_TILE
    pos_spec = pl.BlockSpec((TOKEN_TILE, LANES), lambda i: (i % tiles_per_seq, 0))
    return pl.pallas_call(
        _ffn1_proj_kernel,
        grid=(n_tok // TOKEN_TILE,),
        in_specs=[_tok_spec(D_MODEL), _resident(n1.shape), _resident(wg.shape), _resident(wu.shape),
                  _resident(wd.shape), _resident(mixn.shape), _resident(win.shape), pos_spec, pos_spec],
        out_specs=[_tok_spec(D_MODEL), _tok_spec(768), _tok_spec(768), _tok_spec(768), _tok_spec(MLA_COLS)],
        out_shape=[jax.ShapeDtypeStruct((n_tok, D_MODEL), F32),
                   jax.ShapeDtypeStruct((n_tok, 768), BF16),
                   jax.ShapeDtypeStruct((n_tok, 768), BF16),
                   jax.ShapeDtypeStruct((n_tok, 768), F32),
                   jax.ShapeDtypeStruct((n_tok, MLA_COLS), F32)],
        scratch_shapes=[pltpu.VMEM((TOKEN_TILE, D_MODEL), BF16), pltpu.VMEM((TOKEN_TILE, D_MODEL), F32)],
        compiler_params=_cparams(),
        name="ffn1_proj",
    )(x2, n1, wg, wu, wd, mixn, win, cos, sin)


def _out_ffn2_ple(final, x2, ys, p2, wout, n2, wg, wu, wd, plen, plewg, plewp, fin):
    n_tok = x2.shape[0]
    return pl.pallas_call(
        functools.partial(_out_ffn2_ple_kernel, final),
        grid=(n_tok // TOKEN_TILE,),
        in_specs=[_tok_spec(D_MODEL)] + [_tok_spec(GROUP_WIDTH)] * 4 + [_tok_spec(PLE_DIM)]
        + [_resident(a.shape) for a in (wout, n2, wg, wu, wd, plen, plewg, plewp, fin)],
        out_specs=_tok_spec(D_MODEL),
        out_shape=jax.ShapeDtypeStruct((n_tok, D_MODEL), F32),
        scratch_shapes=[pltpu.VMEM((TOKEN_TILE, D_MODEL), BF16), pltpu.VMEM((TOKEN_TILE, D_MODEL), F32)],
        compiler_params=_cparams(),
        name="out_ffn2_ple",
    )(x2, *ys, p2, wout, n2, wg, wu, wd, plen, plewg, plewp, fin)


def _seq_spec(width):
    return pl.BlockSpec((1, SEQ, width), lambda b: (b, 0, 0))


def _y_shape(batch):
    return jax.ShapeDtypeStruct((batch, SEQ, GROUP_WIDTH), BF16)


def _na_mixer(z, bias_tab, gn):
    batch = z.shape[0]
    return pl.pallas_call(
        _na_kernel, grid=(batch,),
        in_specs=[_seq_spec(768), _resident(bias_tab.shape), _resident(gn.shape)],
        out_specs=_seq_spec(GROUP_WIDTH), out_shape=_y_shape(batch),
        compiler_params=_cparams(), name="na_mixer",
    )(z, bias_tab, gn)


def _swa_mixer(z, sink, gn):
    batch = z.shape[0]
    return pl.pallas_call(
        _swa_kernel, grid=(batch,),
        in_specs=[pl.BlockSpec(memory_space=pltpu.SMEM), _seq_spec(768), _resident(gn.shape)],
        out_specs=_seq_spec(GROUP_WIDTH), out_shape=_y_shape(batch),
        compiler_params=_cparams(), name="swa_mixer",
    )(sink, z, gn)


def _dil_mixer(z, gn):
    batch = z.shape[0]
    pair_specs = [pl.BlockSpec((1, SEQ, LANES), functools.partial(lambda c, b: (b, 0, c), c)) for c in range(6)]
    return pl.pallas_call(
        _dil_kernel, grid=(batch,),
        in_specs=pair_specs + [_resident(gn.shape)],
        out_specs=_seq_spec(GROUP_WIDTH), out_shape=_y_shape(batch),
        scratch_shapes=[pltpu.VMEM((2, SEQ, LANES), F32)] * 3,
        compiler_params=_cparams(), name="dil_mixer",
    )(z, z, z, z, z, z, gn)


def _mla_mixer(z, qn, wuq, kvn, wukv, c_tab, s_tab, gn):
    batch = z.shape[0]
    return pl.pallas_call(
        _mla_kernel, grid=(batch,),
        in_specs=[_seq_spec(MLA_COLS)] + [_resident(a.shape) for a in (qn, wuq, kvn, wukv, c_tab, s_tab, gn)],
        out_specs=_seq_spec(GROUP_WIDTH), out_shape=_y_shape(batch),
        scratch_shapes=[pltpu.VMEM((4, SEQ, LANES), BF16), pltpu.VMEM((4, SEQ, LANES), BF16),
                        pltpu.VMEM((4, SEQ, GROUP_WIDTH), BF16)],
        compiler_params=_cparams(), name="mla_mixer",
    )(z, qn, wuq, kvn, wukv, c_tab, s_tab, gn)


def kernel(x, p, ffn1_norm, ffn1_w_gate, ffn1_w_up, ffn1_w_down, mix_norm, w_in, na_bias, swa_sink, mla_q_norm, mla_w_uq, mla_kv_norm, mla_w_ukv, group_norm, w_out, ffn2_norm, ffn2_w_gate, ffn2_w_up, ffn2_w_down, ple_norm, ple_w_gate, ple_w_proj, final_norm):
    batch, seq, d_model = x.shape
    assert (seq, d_model) == (SEQ, D_MODEL) and x.dtype == F32
    n_tok = batch * seq
    cos64, sin64 = _rope_tables_64()
    c_mla, s_mla = _rope_tables_mla()
    proj_idx, wuq_idx, wukv_idx = _proj_col_index(), _wuq_col_index(), _wukv_col_index()
    row = lambda v: v.reshape(1, -1)

    x2 = x.reshape(n_tok, d_model)
    for i in range(DEPTH):
        win = _gather_cols(w_in[i], proj_idx).astype(BF16)
        x2, z_na, z_swa, z_dil, z_mla = _ffn1_proj(
            x2, row(ffn1_norm[i]), _chunk_cols(ffn1_w_gate[i]), _chunk_cols(ffn1_w_up[i]),
            _chunk_rows(ffn1_w_down[i]), row(mix_norm[i]), win, cos64, sin64)
        seq_view = lambda z: z.reshape(batch, seq, z.shape[-1])
        gn = group_norm[i]
        y_na = _na_mixer(seq_view(z_na), _na_bias_tables(na_bias[i]), row(gn[0]))
        y_swa = _swa_mixer(seq_view(z_swa), swa_sink[i], row(gn[1]))
        y_dil = _dil_mixer(seq_view(z_dil), row(gn[2]))
        y_mla = _mla_mixer(seq_view(z_mla), row(mla_q_norm[i]),
                           _gather_cols(mla_w_uq[i], wuq_idx).astype(BF16), row(mla_kv_norm[i]),
                           _gather_cols(mla_w_ukv[i], wukv_idx).astype(BF16), c_mla, s_mla, row(gn[3]))
        ys = [y.reshape(n_tok, GROUP_WIDTH) for y in (y_na, y_swa, y_dil, y_mla)]
        x2 = _out_ffn2_ple(
            i == DEPTH - 1, x2, ys, p[i].reshape(n_tok, PLE_DIM), w_out[i].astype(BF16),
            row(ffn2_norm[i]), _chunk_cols(ffn2_w_gate[i]), _chunk_cols(ffn2_w_up[i]),
            _chunk_rows(ffn2_w_down[i]), row(ple_norm[i]), ple_w_gate[i].astype(BF16),
            ple_w_proj[i].astype(BF16), row(final_norm))
    return x2.reshape(batch, seq, d_model)
```

```python
import functools
import math

import numpy as np
import jax
import jax.numpy as jnp
from jax import lax
from jax.experimental import pallas as pl
from jax.experimental.pallas import tpu as pltpu

F32 = jnp.float32
BF16 = jnp.bfloat16

LANES = 128
VMEM_LIMIT_BYTES = 58 * 1024 * 1024

D_MODEL = 1024
SEQ = 2048
DEPTH = 4
HEAD_DIM = 64
GROUP_WIDTH = 256
D_FF = 2816
FF_CHUNK = 256
N_FF_CHUNKS = D_FF // FF_CHUNK
PLE_DIM = 256
ROPE_THETA = 10000.0
EPS = 1e-6
NEG_INF = -1e30

GRID_W = 64
NA_ROWS = 8
NA_COLS = 16
NA_KEYS = NA_ROWS * GRID_W
SWA_HALF = 128
SWA_BLOCK = 128
SWA_KEYS = SWA_BLOCK + 2 * SWA_HALF
DIL_PATTERNS = ((128, 1), (512, 4), (2048, 16))
DIL_STEPS = 64
DIL_BLOCK = 128
DIL_KEYS = DIL_BLOCK + 2 * DIL_STEPS
MLA_Q_LORA = 256
MLA_KV_LORA = 128
MLA_NOPE = 64
MLA_ROPE = 32
MLA_QBLOCK = 256

NA_OFF, SWA_OFF, DIL_OFF, MLA_OFF = 0, 768, 1280, 2048
P_NA, P_SWA, P_DIL, P_MLA = 0, 768, 1536, 2304
P_COLS = 2944
MLA_COLS = 640

TOKEN_TILE = 512
BAND_UNROLL = 4


def _cparams(n_axes=1):
    return pltpu.CompilerParams(dimension_semantics=("arbitrary",) * n_axes,
                                vmem_limit_bytes=VMEM_LIMIT_BYTES)


def _resident(shape):
    nd = len(shape)
    return pl.BlockSpec(shape, lambda *_: (0,) * nd, pipeline_mode=pl.Buffered(1))


def _rms(x, g):
    return x * lax.rsqrt(jnp.mean(x * x, axis=-1, keepdims=True) + EPS) * g


def _dot(a, b):
    return jnp.dot(a, b, preferred_element_type=F32)


def _dot_t(a, b):
    return lax.dot_general(a, b, (((1,), (1,)), ((), ())), preferred_element_type=F32)


def _swiglu_half_step(x, norm_ref, wg_ref, wu_ref, wd_ref, u_ref, acc_ref):
    u_ref[...] = _rms(x, norm_ref[...]).astype(BF16)
    acc_ref[...] = jnp.zeros_like(acc_ref)

    for j in range(N_FF_CHUNKS):
        cols = slice(j * FF_CHUNK, (j + 1) * FF_CHUNK)
        u = u_ref[...]
        g = _dot(u, wg_ref[:, cols])
        up = _dot(u, wu_ref[:, cols])
        a = (g * jax.nn.sigmoid(g) * up).astype(BF16)
        acc_ref[...] += _dot(a, wd_ref[cols, :])
    return x + 0.5 * acc_ref[...]


def _rope_tile(x, cos, sin_signed):
    lane = lax.broadcasted_iota(jnp.int32, x.shape, 1)
    first_half = (lane % HEAD_DIM) < (HEAD_DIM // 2)
    partner = jnp.where(first_half, pltpu.roll(x, LANES - HEAD_DIM // 2, 1), pltpu.roll(x, HEAD_DIM // 2, 1))
    return x * cos + partner * sin_signed


def _ffn1_proj_kernel(x_ref, n1_ref, wg_ref, wu_ref, wd_ref, mixn_ref, win_ref, cos_ref, sin_ref,
                      xo_ref, zna_ref, zswa_ref, zdil_ref, zmla_ref, u_ref, acc_ref):
    x = _swiglu_half_step(x_ref[...], n1_ref, wg_ref, wu_ref, wd_ref, u_ref, acc_ref)
    xo_ref[...] = x
    u_ref[...] = _rms(x, mixn_ref[...]).astype(BF16)
    cos = cos_ref[...]
    sin = sin_ref[...]
    q_scale = HEAD_DIM ** -0.5

    def proj(col, width):
        return _dot(u_ref[...], win_ref[:, col:col + width])

    def roped(col, scale):
        z = proj(col, 2 * LANES)
        halves = [_rope_tile(z[:, h * LANES:(h + 1) * LANES], cos, sin) for h in range(2)]
        return jnp.concatenate(halves, axis=1) * scale

    zna_ref[:, 0:256] = (proj(P_NA, 256) * q_scale).astype(BF16)
    zna_ref[:, 256:768] = proj(P_NA + 256, 512).astype(BF16)
    zswa_ref[:, 0:256] = roped(P_SWA, q_scale).astype(BF16)
    zswa_ref[:, 256:512] = roped(P_SWA + 256, 1.0).astype(BF16)
    zswa_ref[:, 512:768] = proj(P_SWA + 512, 256).astype(BF16)
    zdil_ref[:, 0:256] = roped(P_DIL, q_scale)
    zdil_ref[:, 256:512] = roped(P_DIL + 256, 1.0)
    zdil_ref[:, 512:768] = proj(P_DIL + 512, 256)
    zmla_ref[:, 0:384] = proj(P_MLA, 384)
    zmla_ref[:, 384:640] = proj(P_MLA + 384, 256)


def _out_ffn2_ple_kernel(final, x_ref, yna_ref, yswa_ref, ydil_ref, ymla_ref, p_ref, wout_ref,
                         n2_ref, wg_ref, wu_ref, wd_ref, plen_ref, plewg_ref, plewp_ref, fin_ref,
                         xo_ref, u_ref, acc_ref):
    x = x_ref[...]
    mix = _dot(yna_ref[...], wout_ref[0:256, :])
    mix += _dot(yswa_ref[...], wout_ref[256:512, :])
    mix += _dot(ydil_ref[...], wout_ref[512:768, :])
    mix += _dot(ymla_ref[...], wout_ref[768:1024, :])
    x = x + mix
    x = _swiglu_half_step(x, n2_ref, wg_ref, wu_ref, wd_ref, u_ref, acc_ref)
    gate = jax.nn.sigmoid(_dot(_rms(x, plen_ref[...]).astype(BF16), plewg_ref[...]))
    x = x + gate * _dot(p_ref[...].astype(BF16), plewp_ref[...])
    if final:
        x = _rms(x, fin_ref[...])
    xo_ref[...] = x


def _pair_attention(q, k, v, bias2, sink2=None):
    lane = lax.broadcasted_iota(jnp.int32, q.shape, 1)
    zero = jnp.zeros_like(q)
    qq = jnp.concatenate([jnp.where(lane < HEAD_DIM, q, zero), jnp.where(lane >= HEAD_DIM, q, zero)], axis=0)
    s = _dot_t(qq, k) + bias2
    m = jnp.max(s, axis=-1, keepdims=True)
    if sink2 is not None:
        m = jnp.maximum(m, sink2)
    p = jnp.exp(s - m)
    l = jnp.sum(p, axis=-1, keepdims=True)
    if sink2 is not None:
        l = l + jnp.exp(sink2 - m)
    o = _dot(p.astype(BF16), v)
    return m, l, o


def _merge_heads(x2, rows):
    top, bot = x2[:rows], x2[rows:]
    if top.shape[1] == 1:
        top = jnp.broadcast_to(top, (rows, LANES))
        bot = jnp.broadcast_to(bot, (rows, LANES))
    lane = lax.broadcasted_iota(jnp.int32, (rows, LANES), 1)
    return jnp.where(lane < HEAD_DIM, top, bot)


def _band_bias(q_start, k_start, n_q, n_k, half_width):
    qpos = q_start + lax.broadcasted_iota(jnp.int32, (n_q, n_k), 0)
    kpos = k_start + lax.broadcasted_iota(jnp.int32, (n_q, n_k), 1)
    bias = jnp.where(jnp.abs(kpos - qpos) <= half_width, 0.0, NEG_INF).astype(F32)
    return jnp.concatenate([bias, bias], axis=0)


def _na_kernel(z_ref, bias_ref, gn_ref, o_ref):
    rows = SEQ // GRID_W

    def load(r):
        rs = jnp.clip(r - NA_ROWS // 2, 0, rows - NA_ROWS)
        shift = rs - r + NA_ROWS - 1
        q0 = pl.multiple_of(r * GRID_W, GRID_W)
        k0 = pl.multiple_of(rs * GRID_W, GRID_W)
        pairs = []
        for pr in range(2):
            q = z_ref[0, pl.ds(q0, GRID_W), pr * LANES:(pr + 1) * LANES]
            k = z_ref[0, pl.ds(k0, NA_KEYS), 256 + pr * LANES:256 + (pr + 1) * LANES]
            v = z_ref[0, pl.ds(k0, NA_KEYS), 512 + pr * LANES:512 + (pr + 1) * LANES]
            pairs.append((q, k, v, bias_ref[pr, shift]))
        return q0, pairs

    def compute(pairs):
        outs = []
        for q, k, v, bias2 in pairs:
            _, l, o = _pair_attention(q, k, v, bias2)
            outs.append(_merge_heads(o / l, GRID_W))
        return _rms(jnp.concatenate(outs, axis=1), gn_ref[...]).astype(BF16)

    def body(it, carry):
        loaded = [load(it * BAND_UNROLL + u) for u in range(BAND_UNROLL)]
        results = [compute(pairs) for _, pairs in loaded]
        for (q0, _), y in zip(loaded, results):
            o_ref[0, pl.ds(q0, GRID_W), :] = y
        return carry

    lax.fori_loop(0, rows // BAND_UNROLL, body, 0)


def _swa_kernel(sink_ref, z_ref, gn_ref, o_ref):
    head_row = lax.broadcasted_iota(jnp.int32, (2 * SWA_BLOCK, 1), 0) < SWA_BLOCK

    def load(i):
        s0 = pl.multiple_of(i * SWA_BLOCK, SWA_BLOCK)
        ks = pl.multiple_of(jnp.clip(s0 - SWA_HALF, 0, SEQ - SWA_KEYS), SWA_BLOCK)
        pairs = []
        for g in range(2):
            q = z_ref[0, pl.ds(s0, SWA_BLOCK), g * LANES:(g + 1) * LANES]
            k = z_ref[0, pl.ds(ks, SWA_KEYS), 256 + g * LANES:256 + (g + 1) * LANES]
            v = z_ref[0, pl.ds(ks, SWA_KEYS), 512 + g * LANES:512 + (g + 1) * LANES]
            sink2 = jnp.where(head_row, sink_ref[2 * g], sink_ref[2 * g + 1]).astype(F32)
            pairs.append((q, k, v, sink2))
        return s0, ks, pairs

    def compute(s0, ks, pairs):
        bias2 = _band_bias(s0, ks, SWA_BLOCK, SWA_KEYS, SWA_HALF)
        outs = []
        for q, k, v, sink2 in pairs:
            _, l, o = _pair_attention(q, k, v, bias2, sink2)
            outs.append(_merge_heads(o / l, SWA_BLOCK))
        return _rms(jnp.concatenate(outs, axis=1), gn_ref[...]).astype(BF16)

    def body(it, carry):
        loaded = [load(it * BAND_UNROLL + u) for u in range(BAND_UNROLL)]
        results = [compute(*ld) for ld in loaded]
        for (s0, _, _), y in zip(loaded, results):
            o_ref[0, pl.ds(s0, SWA_BLOCK), :] = y
        return carry

    lax.fori_loop(0, SEQ // SWA_BLOCK // BAND_UNROLL, body, 0)


def _dil_kernel(q0_ref, q1_ref, k0_ref, k1_ref, v0_ref, v1_ref, gn_ref, o_ref, m_ref, l_ref, acc_ref):
    qkv_refs = ((q0_ref, k0_ref, v0_ref), (q1_ref, k1_ref, v1_ref))

    def run_pattern(first, dil, steps):
        res_len = SEQ // dil
        blocks_per_res = res_len // DIL_BLOCK
        n_keys = min(DIL_KEYS, res_len)

        def load(idx):
            res = idx // blocks_per_res
            q_lo = (idx % blocks_per_res) * DIL_BLOCK
            k_lo = jnp.clip(q_lo - steps, 0, res_len - n_keys)
            if dil == 1:
                q_rows = pl.ds(pl.multiple_of(q_lo, DIL_BLOCK), DIL_BLOCK)
                k_rows = pl.ds(pl.multiple_of(k_lo, DIL_STEPS), n_keys)
            else:
                q_rows = pl.ds(res + dil * q_lo, DIL_BLOCK, stride=dil)
                k_rows = pl.ds(res + dil * k_lo, n_keys, stride=dil)
            pairs = []
            for pr in range(2):
                q_ref, k_ref, v_ref = qkv_refs[pr]
                qkv = tuple(ref[0, rows, :].astype(BF16)
                            for ref, rows in ((q_ref, q_rows), (k_ref, k_rows), (v_ref, k_rows)))
                old = None if first else (m_ref[pr, q_rows, :], l_ref[pr, q_rows, :], acc_ref[pr, q_rows, :])
                pairs.append((qkv, old))
            return q_lo, k_lo, q_rows, pairs

        def compute(q_lo, k_lo, pairs):
            bias2 = _band_bias(q_lo, k_lo, DIL_BLOCK, n_keys, steps)
            new = []
            for (q, k, v), old in pairs:
                m, l, o = (_merge_heads(t, DIL_BLOCK) for t in _pair_attention(q, k, v, bias2))
                if old is not None:
                    m_old, l_old, o_old = old
                    m_new = jnp.maximum(m_old, m)
                    w_old = jnp.exp(m_old - m_new)
                    w_blk = jnp.exp(m - m_new)
                    m, l, o = m_new, l_old * w_old + l * w_blk, o_old * w_old + o * w_blk
                new.append((m, l, o))
            return new

        def body(it, carry):
            loaded = [load(it * BAND_UNROLL + u) for u in range(BAND_UNROLL)]
            results = [compute(q_lo, k_lo, pairs) for q_lo, k_lo, _, pairs in loaded]
            for (_, _, q_rows, _), new in zip(loaded, results):
                for pr, (m, l, o) in enumerate(new):
                    m_ref[pr, q_rows, :] = m
                    l_ref[pr, q_rows, :] = l
                    acc_ref[pr, q_rows, :] = o
            return carry

        lax.fori_loop(0, dil * blocks_per_res // BAND_UNROLL, body, 0)

    for p_idx, (window, dil) in enumerate(DIL_PATTERNS):
        run_pattern(p_idx == 0, dil, window // 2 // dil)

    y = jnp.concatenate([acc_ref[0] / l_ref[0], acc_ref[1] / l_ref[1]], axis=1)
    o_ref[0] = _rms(y, gn_ref[...]).astype(BF16)


def _mla_kernel(z_ref, qn_ref, wuq_ref, kvn_ref, wukv_ref, c_ref, s_ref, gn_ref, o_ref,
                q_scr, k_scr, v_scr):
    scale = (MLA_NOPE + MLA_ROPE) ** -0.5
    c_tab = c_ref[...]
    s_tab = s_ref[...]
    cqn = _rms(z_ref[0, :, 0:MLA_Q_LORA], qn_ref[...]).astype(BF16)
    for h in range(4):
        g = _dot(cqn, wuq_ref[:, h * 256:(h + 1) * 256])
        q_scr[h] = ((g[:, :LANES] * c_tab + g[:, LANES:] * s_tab) * scale).astype(BF16)
    ckvn = _rms(z_ref[0, :, MLA_Q_LORA:MLA_Q_LORA + MLA_KV_LORA], kvn_ref[...]).astype(BF16)
    k_pe = z_ref[0, :, 384:512] * c_tab + z_ref[0, :, 512:640] * s_tab
    for h in range(4):
        k_scr[h] = (_dot(ckvn, wukv_ref[:, h * LANES:(h + 1) * LANES]) + k_pe).astype(BF16)
    v_all = _dot(ckvn, wukv_ref[:, 512:768])
    lane = lax.broadcasted_iota(jnp.int32, v_all.shape, 1)
    for h in range(4):
        v_scr[h] = jnp.where((lane // HEAD_DIM) == h, v_all, 0.0).astype(BF16)

    def qblock(i, carry):
        q0 = pl.multiple_of(i * MLA_QBLOCK, MLA_QBLOCK)
        y = jnp.zeros((MLA_QBLOCK, GROUP_WIDTH), F32)
        for h in range(4):
            s = _dot_t(q_scr[h, pl.ds(q0, MLA_QBLOCK), :], k_scr[h])
            m = jnp.max(s, axis=-1, keepdims=True)
            p = jnp.exp(s - m)
            l = jnp.sum(p, axis=-1, keepdims=True)
            y = y + _dot(p.astype(BF16), v_scr[h]) / l
        o_ref[0, pl.ds(q0, MLA_QBLOCK), :] = _rms(y, gn_ref[...]).astype(BF16)
        return carry

    lax.fori_loop(0, SEQ // MLA_QBLOCK, qblock, 0)


def _rope_tables_64():
    half = HEAD_DIM // 2
    inv_freq = np.float32(ROPE_THETA) ** (-np.arange(half, dtype=np.float32) / np.float32(half))
    ang = np.arange(SEQ, dtype=np.float32)[:, None] * inv_freq[None, :]
    cos, sin = np.cos(ang).astype(np.float32), np.sin(ang).astype(np.float32)
    cos_h = np.concatenate([cos, cos], axis=1)
    sin_h = np.concatenate([-sin, sin], axis=1)
    return jnp.asarray(np.tile(cos_h, (1, 2))), jnp.asarray(np.tile(sin_h, (1, 2)))


def _rope_tables_mla():
    half = MLA_ROPE // 2
    inv_freq = np.float32(ROPE_THETA) ** (-np.arange(half, dtype=np.float32) / np.float32(half))
    ang = np.arange(SEQ, dtype=np.float32)[:, None] * inv_freq[None, :]
    cos, sin = np.cos(ang).astype(np.float32), np.sin(ang).astype(np.float32)
    ones = np.ones((SEQ, MLA_NOPE), np.float32)
    zeros_n = np.zeros((SEQ, MLA_NOPE), np.float32)
    zeros_p = np.zeros((SEQ, LANES - MLA_NOPE - MLA_ROPE), np.float32)
    c_tab = np.concatenate([ones, cos, cos, zeros_p], axis=1)
    s_tab = np.concatenate([zeros_n, -sin, sin, zeros_p], axis=1)
    return jnp.asarray(c_tab), jnp.asarray(s_tab)


def _cols(w, pieces):
    parts = [jnp.zeros((w.shape[0], pc), w.dtype) if isinstance(pc, int) else w[:, pc[0]:pc[0] + pc[1]]
             for pc in pieces]
    return jnp.concatenate(parts, axis=1)


def _swapped(start, width):
    return [(start + width // 2, width // 2), (start, width // 2)]


def _proj_weight(w_in):
    pad = LANES - MLA_NOPE - MLA_ROPE
    kpe = MLA_OFF + MLA_Q_LORA + MLA_KV_LORA
    k0, k1 = (SWA_OFF + 256, 64), (SWA_OFF + 320, 64)
    v0, v1 = (SWA_OFF + 384, 64), (SWA_OFF + 448, 64)
    pieces = ([(NA_OFF, 768), (SWA_OFF, 256), k0, k0, k1, k1, v0, v0, v1, v1, (DIL_OFF, 768),
               (MLA_OFF, MLA_Q_LORA + MLA_KV_LORA), MLA_NOPE, (kpe, MLA_ROPE), pad, MLA_NOPE]
              + _swapped(kpe, MLA_ROPE) + [pad])
    w = _cols(w_in, pieces).astype(BF16)
    assert w.shape[1] == P_COLS
    return w


def _wuq_weight(w_uq):
    pad = LANES - MLA_NOPE - MLA_ROPE
    pieces = []
    for h in range(4):
        base = h * (MLA_NOPE + MLA_ROPE)
        pieces += [(base, MLA_NOPE + MLA_ROPE), pad, MLA_NOPE] + _swapped(base + MLA_NOPE, MLA_ROPE) + [pad]
    return _cols(w_uq, pieces).astype(BF16)


def _wukv_weight(w_ukv):
    pieces = []
    for h in range(4):
        pieces += [(h * 128, MLA_NOPE), LANES - MLA_NOPE]
    pieces += [(h * 128 + MLA_NOPE, HEAD_DIM) for h in range(4)]
    return _cols(w_ukv, pieces).astype(BF16)


def _na_bias_tables(bias):
    qc = np.arange(GRID_W)[:, None]
    kc = np.arange(GRID_W)[None, :]
    ws = np.clip(qc - NA_COLS // 2, 0, GRID_W - NA_COLS)
    valid = jnp.asarray((kc >= ws) & (kc < ws + NA_COLS))
    lo = GRID_W - NA_COLS
    padded = jnp.pad(bias, ((0, 0), (0, 0), (lo, lo)))
    toep = jnp.stack([padded[:, :, GRID_W - 1 - c:2 * GRID_W - 1 - c] for c in range(GRID_W)], axis=2)
    toep = jnp.where(valid[None, None], toep, NEG_INF)
    g = jnp.stack([toep[:, s:s + NA_ROWS] for s in range(NA_ROWS)], axis=1)
    g = g.transpose(0, 1, 3, 2, 4).reshape(2, 2, NA_ROWS, GRID_W, NA_KEYS)
    return g.transpose(0, 2, 1, 3, 4).reshape(2, NA_ROWS, 2 * GRID_W, NA_KEYS)


def _tok_spec(width):
    return pl.BlockSpec((TOKEN_TILE, width), lambda i: (i, 0))


def _ffn1_proj(x2, n1, wg, wu, wd, mixn, win, cos, sin):
    n_tok = x2.shape[0]
    tiles_per_seq = SEQ // TOKEN_TILE
    pos_spec = pl.BlockSpec((TOKEN_TILE, LANES), lambda i: (i % tiles_per_seq, 0))
    return pl.pallas_call(
        _ffn1_proj_kernel,
        grid=(n_tok // TOKEN_TILE,),
        in_specs=[_tok_spec(D_MODEL), _resident(n1.shape), _resident(wg.shape), _resident(wu.shape),
                  _resident(wd.shape), _resident(mixn.shape), _resident(win.shape), pos_spec, pos_spec],
        out_specs=[_tok_spec(D_MODEL), _tok_spec(768), _tok_spec(768), _tok_spec(768), _tok_spec(MLA_COLS)],
        out_shape=[jax.ShapeDtypeStruct((n_tok, D_MODEL), F32),
                   jax.ShapeDtypeStruct((n_tok, 768), BF16),
                   jax.ShapeDtypeStruct((n_tok, 768), BF16),
                   jax.ShapeDtypeStruct((n_tok, 768), F32),
                   jax.ShapeDtypeStruct((n_tok, MLA_COLS), F32)],
        scratch_shapes=[pltpu.VMEM((TOKEN_TILE, D_MODEL), BF16), pltpu.VMEM((TOKEN_TILE, D_MODEL), F32)],
        compiler_params=_cparams(),
        name="ffn1_proj",
    )(x2, n1, wg, wu, wd, mixn, win, cos, sin)


def _out_ffn2_ple(final, x2, ys, p2, wout, n2, wg, wu, wd, plen, plewg, plewp, fin):
    n_tok = x2.shape[0]
    return pl.pallas_call(
        functools.partial(_out_ffn2_ple_kernel, final),
        grid=(n_tok // TOKEN_TILE,),
        in_specs=[_tok_spec(D_MODEL)] + [_tok_spec(GROUP_WIDTH)] * 4 + [_tok_spec(PLE_DIM)]
        + [_resident(a.shape) for a in (wout, n2, wg, wu, wd, plen, plewg, plewp, fin)],
        out_specs=_tok_spec(D_MODEL),
        out_shape=jax.ShapeDtypeStruct((n_tok, D_MODEL), F32),
        scratch_shapes=[pltpu.VMEM((TOKEN_TILE, D_MODEL), BF16), pltpu.VMEM((TOKEN_TILE, D_MODEL), F32)],
        compiler_params=_cparams(),
        name="out_ffn2_ple",
    )(x2, *ys, p2, wout, n2, wg, wu, wd, plen, plewg, plewp, fin)


def _seq_spec(width):
    return pl.BlockSpec((1, SEQ, width), lambda b: (b, 0, 0))


def _y_shape(batch):
    return jax.ShapeDtypeStruct((batch, SEQ, GROUP_WIDTH), BF16)


def _na_mixer(z, bias_tab, gn):
    batch = z.shape[0]
    return pl.pallas_call(
        _na_kernel, grid=(batch,),
        in_specs=[_seq_spec(768), _resident(bias_tab.shape), _resident(gn.shape)],
        out_specs=_seq_spec(GROUP_WIDTH), out_shape=_y_shape(batch),
        compiler_params=_cparams(), name="na_mixer",
    )(z, bias_tab, gn)


def _swa_mixer(z, sink, gn):
    batch = z.shape[0]
    return pl.pallas_call(
        _swa_kernel, grid=(batch,),
        in_specs=[pl.BlockSpec(memory_space=pltpu.SMEM), _seq_spec(768), _resident(gn.shape)],
        out_specs=_seq_spec(GROUP_WIDTH), out_shape=_y_shape(batch),
        compiler_params=_cparams(), name="swa_mixer",
    )(sink, z, gn)


def _dil_mixer(z, gn):
    batch = z.shape[0]
    pair_specs = [pl.BlockSpec((1, SEQ, LANES), functools.partial(lambda c, b: (b, 0, c), c)) for c in range(6)]
    return pl.pallas_call(
        _dil_kernel, grid=(batch,),
        in_specs=pair_specs + [_resident(gn.shape)],
        out_specs=_seq_spec(GROUP_WIDTH), out_shape=_y_shape(batch),
        scratch_shapes=[pltpu.VMEM((2, SEQ, LANES), F32)] * 3,
        compiler_params=_cparams(), name="dil_mixer",
    )(z, z, z, z, z, z, gn)


def _mla_mixer(z, qn, wuq, kvn, wukv, c_tab, s_tab, gn):
    batch = z.shape[0]
    return pl.pallas_call(
        _mla_kernel, grid=(batch,),
        in_specs=[_seq_spec(MLA_COLS)] + [_resident(a.shape) for a in (qn, wuq, kvn, wukv, c_tab, s_tab, gn)],
        out_specs=_seq_spec(GROUP_WIDTH), out_shape=_y_shape(batch),
        scratch_shapes=[pltpu.VMEM((4, SEQ, LANES), BF16), pltpu.VMEM((4, SEQ, LANES), BF16),
                        pltpu.VMEM((4, SEQ, GROUP_WIDTH), BF16)],
        compiler_params=_cparams(), name="mla_mixer",
    )(z, qn, wuq, kvn, wukv, c_tab, s_tab, gn)


def kernel(x, p, ffn1_norm, ffn1_w_gate, ffn1_w_up, ffn1_w_down, mix_norm, w_in, na_bias, swa_sink, mla_q_norm, mla_w_uq, mla_kv_norm, mla_w_ukv, group_norm, w_out, ffn2_norm, ffn2_w_gate, ffn2_w_up, ffn2_w_down, ple_norm, ple_w_gate, ple_w_proj, final_norm):
    batch, seq, d_model = x.shape
    assert (seq, d_model) == (SEQ, D_MODEL) and x.dtype == F32
    n_tok = batch * seq
    cos64, sin64 = _rope_tables_64()
    c_mla, s_mla = _rope_tables_mla()
    row = lambda v: v.reshape(1, -1)
    bf = lambda w: w.astype(BF16)

    x2 = x.reshape(n_tok, d_model)
    for i in range(DEPTH):
        x2, z_na, z_swa, z_dil, z_mla = _ffn1_proj(
            x2, row(ffn1_norm[i]), bf(ffn1_w_gate[i]), bf(ffn1_w_up[i]), bf(ffn1_w_down[i]),
            row(mix_norm[i]), _proj_weight(w_in[i]), cos64, sin64)
        seq_view = lambda z: z.reshape(batch, seq, z.shape[-1])
        gn = group_norm[i]
        y_na = _na_mixer(seq_view(z_na), _na_bias_tables(na_bias[i]), row(gn[0]))
        y_swa = _swa_mixer(seq_view(z_swa), swa_sink[i], row(gn[1]))
        y_dil = _dil_mixer(seq_view(z_dil), row(gn[2]))
        y_mla = _mla_mixer(seq_view(z_mla), row(mla_q_norm[i]), _wuq_weight(mla_w_uq[i]),
                           row(mla_kv_norm[i]), _wukv_weight(mla_w_ukv[i]), c_mla, s_mla, row(gn[3]))
        ys = [y.reshape(n_tok, GROUP_WIDTH) for y in (y_na, y_swa, y_dil, y_mla)]
        x2 = _out_ffn2_ple(
            i == DEPTH - 1, x2, ys, p[i].reshape(n_tok, PLE_DIM), bf(w_out[i]),
            row(ffn2_norm[i]), bf(ffn2_w_gate[i]), bf(ffn2_w_up[i]), bf(ffn2_w_down[i]),
            row(ple_norm[i]), bf(ple_w_gate[i]), bf(ple_w_proj[i]), row(final_norm))
    return x2.reshape(batch, seq, d_model)
```

```python
import functools
import math

import numpy as np
import jax
import jax.numpy as jnp
from jax import lax
from jax.experimental import pallas as pl
from jax.experimental.pallas import tpu as pltpu

F32 = jnp.float32
BF16 = jnp.bfloat16

LANES = 128
VMEM_LIMIT_BYTES = 58 * 1024 * 1024

D_MODEL = 1024
SEQ = 2048
DEPTH = 4
HEAD_DIM = 64
GROUP_WIDTH = 256
D_FF = 2816
FF_CHUNK = 256
N_FF_CHUNKS = D_FF // FF_CHUNK
PLE_DIM = 256
ROPE_THETA = 10000.0
EPS = 1e-6
NEG_INF = -1e30

GRID_W = 64
NA_ROWS = 8
NA_COLS = 16
NA_KEYS = NA_ROWS * GRID_W
SWA_HALF = 128
SWA_BLOCK = 128
SWA_KEYS = SWA_BLOCK + 2 * SWA_HALF
DIL_PATTERNS = ((128, 1), (512, 4), (2048, 16))
DIL_STEPS = 64
DIL_BLOCK = 128
DIL_KEYS = DIL_BLOCK + 2 * DIL_STEPS
MLA_Q_LORA = 256
MLA_KV_LORA = 128
MLA_NOPE = 64
MLA_ROPE = 32
MLA_QBLOCK = 256

NA_OFF, SWA_OFF, DIL_OFF, MLA_OFF = 0, 768, 1280, 2048
P_NA, P_SWA, P_DIL, P_MLA = 0, 768, 1536, 2304
P_COLS = 2944
MLA_COLS = 640

TOKEN_TILE = 512
BAND_UNROLL = 4


def _cparams(n_axes=1):
    return pltpu.CompilerParams(dimension_semantics=("arbitrary",) * n_axes,
                                vmem_limit_bytes=VMEM_LIMIT_BYTES)


def _resident(shape):
    nd = len(shape)
    return pl.BlockSpec(shape, lambda *_: (0,) * nd, pipeline_mode=pl.Buffered(1))


def _rms(x, g):
    return x * lax.rsqrt(jnp.mean(x * x, axis=-1, keepdims=True) + EPS) * g


def _dot(a, b):
    return jnp.dot(a, b, preferred_element_type=F32)


def _dot_t(a, b):
    return lax.dot_general(a, b, (((1,), (1,)), ((), ())), preferred_element_type=F32)


def _swiglu_half_step(x, norm_ref, wg_ref, wu_ref, wd_ref, u_ref, acc_ref):
    u_ref[...] = _rms(x, norm_ref[...]).astype(BF16)
    acc_ref[...] = jnp.zeros_like(acc_ref)

    for j in range(N_FF_CHUNKS):
        cols = slice(j * FF_CHUNK, (j + 1) * FF_CHUNK)
        u = u_ref[...]
        g = _dot(u, wg_ref[:, cols])
        up = _dot(u, wu_ref[:, cols])
        a = (g * jax.nn.sigmoid(g) * up).astype(BF16)
        acc_ref[...] += _dot(a, wd_ref[cols, :])
    return x + 0.5 * acc_ref[...]


def _rope_tile(x, cos, sin_signed):
    lane = lax.broadcasted_iota(jnp.int32, x.shape, 1)
    first_half = (lane % HEAD_DIM) < (HEAD_DIM // 2)
    partner = jnp.where(first_half, pltpu.roll(x, LANES - HEAD_DIM // 2, 1), pltpu.roll(x, HEAD_DIM // 2, 1))
    return x * cos + partner * sin_signed


def _ffn1_proj_kernel(x_ref, n1_ref, wg_ref, wu_ref, wd_ref, mixn_ref, win_ref, cos_ref, sin_ref,
                      xo_ref, zna_ref, zswa_ref, zdil_ref, zmla_ref, u_ref, acc_ref):
    x = _swiglu_half_step(x_ref[...], n1_ref, wg_ref, wu_ref, wd_ref, u_ref, acc_ref)
    xo_ref[...] = x
    u_ref[...] = _rms(x, mixn_ref[...]).astype(BF16)
    cos = cos_ref[...]
    sin = sin_ref[...]
    q_scale = HEAD_DIM ** -0.5

    def proj(col, width):
        return _dot(u_ref[...], win_ref[:, col:col + width])

    def roped(col, scale):
        z = proj(col, 2 * LANES)
        halves = [_rope_tile(z[:, h * LANES:(h + 1) * LANES], cos, sin) for h in range(2)]
        return jnp.concatenate(halves, axis=1) * scale

    zna_ref[:, 0:256] = (proj(P_NA, 256) * q_scale).astype(BF16)
    zna_ref[:, 256:768] = proj(P_NA + 256, 512).astype(BF16)
    zswa_ref[:, 0:256] = roped(P_SWA, q_scale).astype(BF16)
    zswa_ref[:, 256:512] = roped(P_SWA + 256, 1.0).astype(BF16)
    zswa_ref[:, 512:768] = proj(P_SWA + 512, 256).astype(BF16)
    zdil_ref[:, 0:256] = roped(P_DIL, q_scale)
    zdil_ref[:, 256:512] = roped(P_DIL + 256, 1.0)
    zdil_ref[:, 512:768] = proj(P_DIL + 512, 256)
    zmla_ref[:, 0:384] = proj(P_MLA, 384)
    zmla_ref[:, 384:640] = proj(P_MLA + 384, 256)


def _out_ffn2_ple_kernel(final, x_ref, yna_ref, yswa_ref, ydil_ref, ymla_ref, p_ref, wout_ref,
                         n2_ref, wg_ref, wu_ref, wd_ref, plen_ref, plewg_ref, plewp_ref, fin_ref,
                         xo_ref, u_ref, acc_ref):
    x = x_ref[...]
    mix = _dot(yna_ref[...], wout_ref[0:256, :])
    mix += _dot(yswa_ref[...], wout_ref[256:512, :])
    mix += _dot(ydil_ref[...], wout_ref[512:768, :])
    mix += _dot(ymla_ref[...], wout_ref[768:1024, :])
    x = x + mix
    x = _swiglu_half_step(x, n2_ref, wg_ref, wu_ref, wd_ref, u_ref, acc_ref)
    gate = jax.nn.sigmoid(_dot(_rms(x, plen_ref[...]).astype(BF16), plewg_ref[...]))
    x = x + gate * _dot(p_ref[...].astype(BF16), plewp_ref[...])
    if final:
        x = _rms(x, fin_ref[...])
    xo_ref[...] = x


def _pair_attention(items):
    def stack_heads(q):
        lane = lax.broadcasted_iota(jnp.int32, q.shape, 1)
        zero = jnp.zeros_like(q)
        return jnp.concatenate([jnp.where(lane < HEAD_DIM, q, zero), jnp.where(lane >= HEAD_DIM, q, zero)], axis=0)

    ss = [_dot_t(stack_heads(q), k) + bias2 for q, k, _, bias2, _ in items]
    ms = [jnp.max(s, axis=-1, keepdims=True) for s in ss]
    ms = [m if it[4] is None else jnp.maximum(m, it[4]) for m, it in zip(ms, items)]
    ps = [jnp.exp(s - m) for s, m in zip(ss, ms)]
    ls = [jnp.sum(p, axis=-1, keepdims=True) for p in ps]
    ls = [l if it[4] is None else l + jnp.exp(it[4] - m) for l, m, it in zip(ls, ms, items)]
    os_ = [_dot(p.astype(BF16), it[2]) for p, it in zip(ps, items)]
    return list(zip(ms, ls, os_))


def _merge_heads(x2, rows):
    top, bot = x2[:rows], x2[rows:]
    if top.shape[1] == 1:
        top = jnp.broadcast_to(top, (rows, LANES))
        bot = jnp.broadcast_to(bot, (rows, LANES))
    lane = lax.broadcasted_iota(jnp.int32, (rows, LANES), 1)
    return jnp.where(lane < HEAD_DIM, top, bot)


def _band_bias(q_start, k_start, n_q, n_k, half_width):
    qpos = q_start + lax.broadcasted_iota(jnp.int32, (n_q, n_k), 0)
    kpos = k_start + lax.broadcasted_iota(jnp.int32, (n_q, n_k), 1)
    bias = jnp.where(jnp.abs(kpos - qpos) <= half_width, 0.0, NEG_INF).astype(F32)
    return jnp.concatenate([bias, bias], axis=0)


def _na_kernel(z_ref, bias_ref, gn_ref, o_ref):
    rows = SEQ // GRID_W

    def load(r):
        rs = jnp.clip(r - NA_ROWS // 2, 0, rows - NA_ROWS)
        shift = rs - r + NA_ROWS - 1
        q0 = pl.multiple_of(r * GRID_W, GRID_W)
        k0 = pl.multiple_of(rs * GRID_W, GRID_W)
        pairs = []
        for pr in range(2):
            q = z_ref[0, pl.ds(q0, GRID_W), pr * LANES:(pr + 1) * LANES]
            k = z_ref[0, pl.ds(k0, NA_KEYS), 256 + pr * LANES:256 + (pr + 1) * LANES]
            v = z_ref[0, pl.ds(k0, NA_KEYS), 512 + pr * LANES:512 + (pr + 1) * LANES]
            pairs.append((q, k, v, bias_ref[pr, shift], None))
        return q0, pairs

    def body(it, carry):
        loaded = [load(it * BAND_UNROLL + u) for u in range(BAND_UNROLL)]
        stats = _pair_attention([item for _, pairs in loaded for item in pairs])
        outs = [_merge_heads(o / l, GRID_W) for _, l, o in stats]
        ys = [_rms(jnp.concatenate(outs[2 * u:2 * u + 2], axis=1), gn_ref[...]).astype(BF16)
              for u in range(BAND_UNROLL)]
        for (q0, _), y in zip(loaded, ys):
            o_ref[0, pl.ds(q0, GRID_W), :] = y
        return carry

    lax.fori_loop(0, rows // BAND_UNROLL, body, 0)


def _swa_kernel(sink_ref, z_ref, gn_ref, o_ref):
    head_row = lax.broadcasted_iota(jnp.int32, (2 * SWA_BLOCK, 1), 0) < SWA_BLOCK

    def load(i):
        s0 = pl.multiple_of(i * SWA_BLOCK, SWA_BLOCK)
        ks = pl.multiple_of(jnp.clip(s0 - SWA_HALF, 0, SEQ - SWA_KEYS), SWA_BLOCK)
        bias2 = _band_bias(s0, ks, SWA_BLOCK, SWA_KEYS, SWA_HALF)
        pairs = []
        for g in range(2):
            q = z_ref[0, pl.ds(s0, SWA_BLOCK), g * LANES:(g + 1) * LANES]
            k = z_ref[0, pl.ds(ks, SWA_KEYS), 256 + g * LANES:256 + (g + 1) * LANES]
            v = z_ref[0, pl.ds(ks, SWA_KEYS), 512 + g * LANES:512 + (g + 1) * LANES]
            sink2 = jnp.where(head_row, sink_ref[2 * g], sink_ref[2 * g + 1]).astype(F32)
            pairs.append((q, k, v, bias2, sink2))
        return s0, pairs

    def body(it, carry):
        loaded = [load(it * BAND_UNROLL + u) for u in range(BAND_UNROLL)]
        stats = _pair_attention([item for _, pairs in loaded for item in pairs])
        outs = [_merge_heads(o / l, SWA_BLOCK) for _, l, o in stats]
        ys = [_rms(jnp.concatenate(outs[2 * u:2 * u + 2], axis=1), gn_ref[...]).astype(BF16)
              for u in range(BAND_UNROLL)]
        for (s0, _), y in zip(loaded, ys):
            o_ref[0, pl.ds(s0, SWA_BLOCK), :] = y
        return carry

    lax.fori_loop(0, SEQ // SWA_BLOCK // BAND_UNROLL, body, 0)


def _dil_kernel(q0_ref, q1_ref, k0_ref, k1_ref, v0_ref, v1_ref, gn_ref, o_ref, m_ref, l_ref, acc_ref):
    qkv_refs = ((q0_ref, k0_ref, v0_ref), (q1_ref, k1_ref, v1_ref))

    def run_pattern(first, dil, steps):
        res_len = SEQ // dil
        blocks_per_res = res_len // DIL_BLOCK
        n_keys = min(DIL_KEYS, res_len)

        def load(idx):
            res = idx // blocks_per_res
            q_lo = (idx % blocks_per_res) * DIL_BLOCK
            k_lo = jnp.clip(q_lo - steps, 0, res_len - n_keys)
            if dil == 1:
                q_rows = pl.ds(pl.multiple_of(q_lo, DIL_BLOCK), DIL_BLOCK)
                k_rows = pl.ds(pl.multiple_of(k_lo, DIL_STEPS), n_keys)
            else:
                q_rows = pl.ds(res + dil * q_lo, DIL_BLOCK, stride=dil)
                k_rows = pl.ds(res + dil * k_lo, n_keys, stride=dil)
            bias2 = _band_bias(q_lo, k_lo, DIL_BLOCK, n_keys, steps)
            items, olds = [], []
            for pr in range(2):
                q_ref, k_ref, v_ref = qkv_refs[pr]
                q, k, v = (ref[0, rows, :].astype(BF16)
                           for ref, rows in ((q_ref, q_rows), (k_ref, k_rows), (v_ref, k_rows)))
                items.append((q, k, v, bias2, None))
                olds.append(None if first else (m_ref[pr, q_rows, :], l_ref[pr, q_rows, :], acc_ref[pr, q_rows, :]))
            return q_rows, items, olds

        def merge(new, old):
            m, l, o = (_merge_heads(t, DIL_BLOCK) for t in new)
            if old is None:
                return m, l, o
            m_old, l_old, o_old = old
            m_new = jnp.maximum(m_old, m)
            w_old = jnp.exp(m_old - m_new)
            w_blk = jnp.exp(m - m_new)
            return m_new, l_old * w_old + l * w_blk, o_old * w_old + o * w_blk

        def body(it, carry):
            loaded = [load(it * BAND_UNROLL + u) for u in range(BAND_UNROLL)]
            stats = _pair_attention([item for _, items, _ in loaded for item in items])
            olds = [old for _, _, olds in loaded for old in olds]
            merged = [merge(new, old) for new, old in zip(stats, olds)]
            for u, (q_rows, _, _) in enumerate(loaded):
                for pr in range(2):
                    m, l, o = merged[2 * u + pr]
                    m_ref[pr, q_rows, :] = m
                    l_ref[pr, q_rows, :] = l
                    acc_ref[pr, q_rows, :] = o
            return carry

        lax.fori_loop(0, dil * blocks_per_res // BAND_UNROLL, body, 0)

    for p_idx, (window, dil) in enumerate(DIL_PATTERNS):
        run_pattern(p_idx == 0, dil, window // 2 // dil)

    y = jnp.concatenate([acc_ref[0] / l_ref[0], acc_ref[1] / l_ref[1]], axis=1)
    o_ref[0] = _rms(y, gn_ref[...]).astype(BF16)


def _mla_kernel(z_ref, qn_ref, wuq_ref, kvn_ref, wukv_ref, c_ref, s_ref, gn_ref, o_ref,
                q_scr, k_scr, v_scr, sa_scr, sb_scr, ma_scr, mb_scr):
    scale = (MLA_NOPE + MLA_ROPE) ** -0.5
    c_tab = c_ref[...]
    s_tab = s_ref[...]
    cqn = _rms(z_ref[0, :, 0:MLA_Q_LORA], qn_ref[...]).astype(BF16)
    for h in range(4):
        g = _dot(cqn, wuq_ref[:, h * 256:(h + 1) * 256])
        q_scr[h] = ((g[:, :LANES] * c_tab + g[:, LANES:] * s_tab) * scale).astype(BF16)
    ckvn = _rms(z_ref[0, :, MLA_Q_LORA:MLA_Q_LORA + MLA_KV_LORA], kvn_ref[...]).astype(BF16)
    k_pe = z_ref[0, :, 384:512] * c_tab + z_ref[0, :, 512:640] * s_tab
    for h in range(4):
        k_scr[h] = (_dot(ckvn, wukv_ref[:, h * LANES:(h + 1) * LANES]) + k_pe).astype(BF16)
    v_all = _dot(ckvn, wukv_ref[:, 512:768])
    lane = lax.broadcasted_iota(jnp.int32, v_all.shape, 1)
    for h in range(4):
        v_scr[h] = jnp.where((lane // HEAD_DIM) == h, v_all, 0.0).astype(BF16)

    heads = range(4)
    n_blocks = SEQ // MLA_QBLOCK

    def scores(h, blk, s_scr, m_scr):
        q0 = pl.multiple_of(blk * MLA_QBLOCK, MLA_QBLOCK)
        s = _dot_t(q_scr[h, pl.ds(q0, MLA_QBLOCK), :], k_scr[h])
        s_scr[h] = s
        m_scr[h] = jnp.broadcast_to(jnp.max(s, axis=-1, keepdims=True), (MLA_QBLOCK, LANES))

    def finish(cur, blk, nxt, nxt_blk):
        ys = []
        for h in heads:
            scores(h, nxt_blk, *nxt)
            s_scr, m_scr = cur
            p = jnp.exp(s_scr[h] - jnp.concatenate([m_scr[h]] * (SEQ // LANES), axis=1))
            l = jnp.sum(p, axis=-1, keepdims=True)
            ys.append(_dot(p.astype(BF16), v_scr[h]) / l)
        y = (ys[0] + ys[1]) + (ys[2] + ys[3])
        q0 = pl.multiple_of(blk * MLA_QBLOCK, MLA_QBLOCK)
        o_ref[0, pl.ds(q0, MLA_QBLOCK), :] = _rms(y, gn_ref[...]).astype(BF16)

    buf_a, buf_b = (sa_scr, ma_scr), (sb_scr, mb_scr)
    for h in heads:
        scores(h, 0, *buf_a)

    def two_blocks(j, carry):
        finish(buf_a, 2 * j, buf_b, 2 * j + 1)
        finish(buf_b, 2 * j + 1, buf_a, (2 * j + 2) % n_blocks)
        return carry

    lax.fori_loop(0, n_blocks // 2, two_blocks, 0)


def _rope_tables_64():
    half = HEAD_DIM // 2
    inv_freq = np.float32(ROPE_THETA) ** (-np.arange(half, dtype=np.float32) / np.float32(half))
    ang = np.arange(SEQ, dtype=np.float32)[:, None] * inv_freq[None, :]
    cos, sin = np.cos(ang).astype(np.float32), np.sin(ang).astype(np.float32)
    cos_h = np.concatenate([cos, cos], axis=1)
    sin_h = np.concatenate([-sin, sin], axis=1)
    return jnp.asarray(np.tile(cos_h, (1, 2))), jnp.asarray(np.tile(sin_h, (1, 2)))


def _rope_tables_mla():
    half = MLA_ROPE // 2
    inv_freq = np.float32(ROPE_THETA) ** (-np.arange(half, dtype=np.float32) / np.float32(half))
    ang = np.arange(SEQ, dtype=np.float32)[:, None] * inv_freq[None, :]
    cos, sin = np.cos(ang).astype(np.float32), np.sin(ang).astype(np.float32)
    ones = np.ones((SEQ, MLA_NOPE), np.float32)
    zeros_n = np.zeros((SEQ, MLA_NOPE), np.float32)
    zeros_p = np.zeros((SEQ, LANES - MLA_NOPE - MLA_ROPE), np.float32)
    c_tab = np.concatenate([ones, cos, cos, zeros_p], axis=1)
    s_tab = np.concatenate([zeros_n, -sin, sin, zeros_p], axis=1)
    return jnp.asarray(c_tab), jnp.asarray(s_tab)


def _cols(w, pieces):
    parts = [jnp.zeros((w.shape[0], pc), w.dtype) if isinstance(pc, int) else w[:, pc[0]:pc[0] + pc[1]]
             for pc in pieces]
    return jnp.concatenate(parts, axis=1)


def _swapped(start, width):
    return [(start + width // 2, width // 2), (start, width // 2)]


def _proj_weight(w_in):
    pad = LANES - MLA_NOPE - MLA_ROPE
    kpe = MLA_OFF + MLA_Q_LORA + MLA_KV_LORA
    k0, k1 = (SWA_OFF + 256, 64), (SWA_OFF + 320, 64)
    v0, v1 = (SWA_OFF + 384, 64), (SWA_OFF + 448, 64)
    pieces = ([(NA_OFF, 768), (SWA_OFF, 256), k0, k0, k1, k1, v0, v0, v1, v1, (DIL_OFF, 768),
               (MLA_OFF, MLA_Q_LORA + MLA_KV_LORA), MLA_NOPE, (kpe, MLA_ROPE), pad, MLA_NOPE]
              + _swapped(kpe, MLA_ROPE) + [pad])
    w = _cols(w_in, pieces).astype(BF16)
    assert w.shape[1] == P_COLS
    return w


def _wuq_weight(w_uq):
    pad = LANES - MLA_NOPE - MLA_ROPE
    pieces = []
    for h in range(4):
        base = h * (MLA_NOPE + MLA_ROPE)
        pieces += [(base, MLA_NOPE + MLA_ROPE), pad, MLA_NOPE] + _swapped(base + MLA_NOPE, MLA_ROPE) + [pad]
    return _cols(w_uq, pieces).astype(BF16)


def _wukv_weight(w_ukv):
    pieces = []
    for h in range(4):
        pieces += [(h * 128, MLA_NOPE), LANES - MLA_NOPE]
    pieces += [(h * 128 + MLA_NOPE, HEAD_DIM) for h in range(4)]
    return _cols(w_ukv, pieces).astype(BF16)


def _na_bias_tables(bias):
    qc = np.arange(GRID_W)[:, None]
    kc = np.arange(GRID_W)[None, :]
    ws = np.clip(qc - NA_COLS // 2, 0, GRID_W - NA_COLS)
    valid = jnp.asarray((kc >= ws) & (kc < ws + NA_COLS))
    lo = GRID_W - NA_COLS
    padded = jnp.pad(bias, ((0, 0), (0, 0), (lo, lo)))
    toep = jnp.stack([padded[:, :, GRID_W - 1 - c:2 * GRID_W - 1 - c] for c in range(GRID_W)], axis=2)
    toep = jnp.where(valid[None, None], toep, NEG_INF)
    g = jnp.stack([toep[:, s:s + NA_ROWS] for s in range(NA_ROWS)], axis=1)
    g = g.transpose(0, 1, 3, 2, 4).reshape(2, 2, NA_ROWS, GRID_W, NA_KEYS)
    return g.transpose(0, 2, 1, 3, 4).reshape(2, NA_ROWS, 2 * GRID_W, NA_KEYS)


def _tok_spec(width):
    return pl.BlockSpec((TOKEN_TILE, width), lambda i: (i, 0))


def _ffn1_proj(x2, n1, wg, wu, wd, mixn, win, cos, sin):
    n_tok = x2.shape[0]
    tiles_per_seq = SEQ // TOKEN_TILE
    pos_spec = pl.BlockSpec((TOKEN_TILE, LANES), lambda i: (i % tiles_per_seq, 0))
    return pl.pallas_call(
        _ffn1_proj_kernel,
        grid=(n_tok // TOKEN_TILE,),
        in_specs=[_tok_spec(D_MODEL), _resident(n1.shape), _resident(wg.shape), _resident(wu.shape),
                  _resident(wd.shape), _resident(mixn.shape), _resident(win.shape), pos_spec, pos_spec],
        out_specs=[_tok_spec(D_MODEL), _tok_spec(768), _tok_spec(768), _tok_spec(768), _tok_spec(MLA_COLS)],
        out_shape=[jax.ShapeDtypeStruct((n_tok, D_MODEL), F32),
                   jax.ShapeDtypeStruct((n_tok, 768), BF16),
                   jax.ShapeDtypeStruct((n_tok, 768), BF16),
                   jax.ShapeDtypeStruct((n_tok, 768), F32),
                   jax.ShapeDtypeStruct((n_tok, MLA_COLS), F32)],
        scratch_shapes=[pltpu.VMEM((TOKEN_TILE, D_MODEL), BF16), pltpu.VMEM((TOKEN_TILE, D_MODEL), F32)],
        compiler_params=_cparams(),
        name="ffn1_proj",
    )(x2, n1, wg, wu, wd, mixn, win, cos, sin)


def _out_ffn2_ple(final, x2, ys, p2, wout, n2, wg, wu, wd, plen, plewg, plewp, fin):
    n_tok = x2.shape[0]
    return pl.pallas_call(
        functools.partial(_out_ffn2_ple_kernel, final),
        grid=(n_tok // TOKEN_TILE,),
        in_specs=[_tok_spec(D_MODEL)] + [_tok_spec(GROUP_WIDTH)] * 4 + [_tok_spec(PLE_DIM)]
        + [_resident(a.shape) for a in (wout, n2, wg, wu, wd, plen, plewg, plewp, fin)],
        out_specs=_tok_spec(D_MODEL),
        out_shape=jax.ShapeDtypeStruct((n_tok, D_MODEL), F32),
        scratch_shapes=[pltpu.VMEM((TOKEN_TILE, D_MODEL), BF16), pltpu.VMEM((TOKEN_TILE, D_MODEL), F32)],
        compiler_params=_cparams(),
        name="out_ffn2_ple",
    )(x2, *ys, p2, wout, n2, wg, wu, wd, plen, plewg, plewp, fin)


def _seq_spec(width):
    return pl.BlockSpec((1, SEQ, width), lambda b: (b, 0, 0))


def _y_shape(batch):
    return jax.ShapeDtypeStruct((batch, SEQ, GROUP_WIDTH), BF16)


def _na_mixer(z, bias_tab, gn):
    batch = z.shape[0]
    return pl.pallas_call(
        _na_kernel, grid=(batch,),
        in_specs=[_seq_spec(768), _resident(bias_tab.shape), _resident(gn.shape)],
        out_specs=_seq_spec(GROUP_WIDTH), out_shape=_y_shape(batch),
        compiler_params=_cparams(), name="na_mixer",
    )(z, bias_tab, gn)


def _swa_mixer(z, sink, gn):
    batch = z.shape[0]
    return pl.pallas_call(
        _swa_kernel, grid=(batch,),
        in_specs=[pl.BlockSpec(memory_space=pltpu.SMEM), _seq_spec(768), _resident(gn.shape)],
        out_specs=_seq_spec(GROUP_WIDTH), out_shape=_y_shape(batch),
        compiler_params=_cparams(), name="swa_mixer",
    )(sink, z, gn)


def _dil_mixer(z, gn):
    batch = z.shape[0]
    pair_specs = [pl.BlockSpec((1, SEQ, LANES), functools.partial(lambda c, b: (b, 0, c), c)) for c in range(6)]
    return pl.pallas_call(
        _dil_kernel, grid=(batch,),
        in_specs=pair_specs + [_resident(gn.shape)],
        out_specs=_seq_spec(GROUP_WIDTH), out_shape=_y_shape(batch),
        scratch_shapes=[pltpu.VMEM((2, SEQ, LANES), F32)] * 3,
        compiler_params=_cparams(), name="dil_mixer",
    )(z, z, z, z, z, z, gn)


def _mla_mixer(z, qn, wuq, kvn, wukv, c_tab, s_tab, gn):
    batch = z.shape[0]
    return pl.pallas_call(
        _mla_kernel, grid=(batch,),
        in_specs=[_seq_spec(MLA_COLS)] + [_resident(a.shape) for a in (qn, wuq, kvn, wukv, c_tab, s_tab, gn)],
        out_specs=_seq_spec(GROUP_WIDTH), out_shape=_y_shape(batch),
        scratch_shapes=[pltpu.VMEM((4, SEQ, LANES), BF16), pltpu.VMEM((4, SEQ, LANES), BF16),
                        pltpu.VMEM((4, SEQ, GROUP_WIDTH), BF16),
                        pltpu.VMEM((4, MLA_QBLOCK, SEQ), F32), pltpu.VMEM((4, MLA_QBLOCK, SEQ), F32),
                        pltpu.VMEM((4, MLA_QBLOCK, LANES), F32), pltpu.VMEM((4, MLA_QBLOCK, LANES), F32)],
        compiler_params=_cparams(), name="mla_mixer",
    )(z, qn, wuq, kvn, wukv, c_tab, s_tab, gn)


def kernel(x, p, ffn1_norm, ffn1_w_gate, ffn1_w_up, ffn1_w_down, mix_norm, w_in, na_bias, swa_sink, mla_q_norm, mla_w_uq, mla_kv_norm, mla_w_ukv, group_norm, w_out, ffn2_norm, ffn2_w_gate, ffn2_w_up, ffn2_w_down, ple_norm, ple_w_gate, ple_w_proj, final_norm):
    batch, seq, d_model = x.shape
    assert (seq, d_model) == (SEQ, D_MODEL) and x.dtype == F32
    n_tok = batch * seq
    cos64, sin64 = _rope_tables_64()
    c_mla, s_mla = _rope_tables_mla()
    row = lambda v: v.reshape(1, -1)
    bf = lambda w: w.astype(BF16)

    x2 = x.reshape(n_tok, d_model)
    for i in range(DEPTH):
        x2, z_na, z_swa, z_dil, z_mla = _ffn1_proj(
            x2, row(ffn1_norm[i]), bf(ffn1_w_gate[i]), bf(ffn1_w_up[i]), bf(ffn1_w_down[i]),
            row(mix_norm[i]), _proj_weight(w_in[i]), cos64, sin64)
        seq_view = lambda z: z.reshape(batch, seq, z.shape[-1])
        gn = group_norm[i]
        y_na = _na_mixer(seq_view(z_na), _na_bias_tables(na_bias[i]), row(gn[0]))
        y_swa = _swa_mixer(seq_view(z_swa), swa_sink[i], row(gn[1]))
        y_dil = _dil_mixer(seq_view(z_dil), row(gn[2]))
        y_mla = _mla_mixer(seq_view(z_mla), row(mla_q_norm[i]), _wuq_weight(mla_w_uq[i]),
                           row(mla_kv_norm[i]), _wukv_weight(mla_w_ukv[i]), c_mla, s_mla, row(gn[3]))
        ys = [y.reshape(n_tok, GROUP_WIDTH) for y in (y_na, y_swa, y_dil, y_mla)]
        x2 = _out_ffn2_ple(
            i == DEPTH - 1, x2, ys, p[i].reshape(n_tok, PLE_DIM), bf(w_out[i]),
            row(ffn2_norm[i]), bf(ffn2_w_gate[i]), bf(ffn2_w_up[i]), bf(ffn2_w_down[i]),
            row(ple_norm[i]), bf(ple_w_gate[i]), bf(ple_w_proj[i]), row(final_norm))
    return x2.reshape(batch, seq, d_model)
```

```python
import functools
import math

import numpy as np
import jax
import jax.numpy as jnp
from jax import lax
from jax.experimental import pallas as pl
from jax.experimental.pallas import tpu as pltpu

F32 = jnp.float32
BF16 = jnp.bfloat16

LANES = 128
VMEM_LIMIT_BYTES = 58 * 1024 * 1024

D_MODEL = 1024
SEQ = 2048
DEPTH = 4
HEAD_DIM = 64
GROUP_WIDTH = 256
D_FF = 2816
FF_CHUNK = 256
N_FF_CHUNKS = D_FF // FF_CHUNK
PLE_DIM = 256
ROPE_THETA = 10000.0
EPS = 1e-6
NEG_INF = -1e30

GRID_W = 64
NA_ROWS = 8
NA_COLS = 16
NA_KEYS = NA_ROWS * GRID_W
SWA_HALF = 128
SWA_BLOCK = 128
SWA_KEYS = SWA_BLOCK + 2 * SWA_HALF
DIL_PATTERNS = ((128, 1), (512, 4), (2048, 16))
DIL_STEPS = 64
DIL_BLOCK = 128
DIL_KEYS = DIL_BLOCK + 2 * DIL_STEPS
MLA_Q_LORA = 256
MLA_KV_LORA = 128
MLA_NOPE = 64
MLA_ROPE = 32
MLA_QBLOCK = 256

NA_OFF, SWA_OFF, DIL_OFF, MLA_OFF = 0, 768, 1280, 2048
P_NA, P_SWA, P_DIL, P_MLA = 0, 768, 1536, 2304
P_COLS = 2944
MLA_COLS = 640

TOKEN_TILE = 512
BAND_UNROLL = 4


def _cparams(n_axes=1):
    return pltpu.CompilerParams(dimension_semantics=("arbitrary",) * n_axes,
                                vmem_limit_bytes=VMEM_LIMIT_BYTES)


def _resident(param):
    arr, lead = param
    rest = arr.shape[len(lead):]
    return pl.BlockSpec((None,) * len(lead) + rest, lambda *_: lead + (0,) * len(rest),
                        pipeline_mode=pl.Buffered(1))


def _rms(x, g):
    return x * lax.rsqrt(jnp.mean(x * x, axis=-1, keepdims=True) + EPS) * g


def _dot(a, b):
    return jnp.dot(a, b, preferred_element_type=F32)


def _dot_t(a, b):
    return lax.dot_general(a, b, (((1,), (1,)), ((), ())), preferred_element_type=F32)


def _swiglu_half_step(x, norm_ref, wg_ref, wu_ref, wd_ref, u_ref, acc_ref):
    u_ref[...] = _rms(x, norm_ref[...]).astype(BF16)
    acc_ref[...] = jnp.zeros_like(acc_ref)

    for j in range(N_FF_CHUNKS):
        cols = slice(j * FF_CHUNK, (j + 1) * FF_CHUNK)
        u = u_ref[...]
        g = _dot(u, wg_ref[:, cols])
        up = _dot(u, wu_ref[:, cols])
        a = (g * jax.nn.sigmoid(g) * up).astype(BF16)
        acc_ref[...] += _dot(a, wd_ref[cols, :])
    return x + 0.5 * acc_ref[...]


def _rope_tile(x, cos, sin_signed):
    lane = lax.broadcasted_iota(jnp.int32, x.shape, 1)
    first_half = (lane % HEAD_DIM) < (HEAD_DIM // 2)
    partner = jnp.where(first_half, pltpu.roll(x, LANES - HEAD_DIM // 2, 1), pltpu.roll(x, HEAD_DIM // 2, 1))
    return x * cos + partner * sin_signed


def _ffn1_proj_kernel(x_ref, n1_ref, wg_ref, wu_ref, wd_ref, mixn_ref, win_ref, cos_ref, sin_ref,
                      xo_ref, zna_ref, zswa_ref, zdil_ref, zmla_ref, u_ref, acc_ref):
    x = _swiglu_half_step(x_ref[...], n1_ref, wg_ref, wu_ref, wd_ref, u_ref, acc_ref)
    xo_ref[...] = x
    u_ref[...] = _rms(x, mixn_ref[...]).astype(BF16)
    cos = cos_ref[...]
    sin = sin_ref[...]
    q_scale = HEAD_DIM ** -0.5

    def proj(col, width):
        return _dot(u_ref[...], win_ref[:, col:col + width])

    def roped(col, scale):
        z = proj(col, 2 * LANES)
        halves = [_rope_tile(z[:, h * LANES:(h + 1) * LANES], cos, sin) for h in range(2)]
        return jnp.concatenate(halves, axis=1) * scale

    zna_ref[:, 0:256] = (proj(P_NA, 256) * q_scale).astype(BF16)
    zna_ref[:, 256:768] = proj(P_NA + 256, 512).astype(BF16)
    zswa_ref[:, 0:256] = roped(P_SWA, q_scale).astype(BF16)
    zswa_ref[:, 256:512] = roped(P_SWA + 256, 1.0).astype(BF16)
    zswa_ref[:, 512:768] = proj(P_SWA + 512, 256).astype(BF16)
    zdil_ref[:, 0:256] = roped(P_DIL, q_scale)
    zdil_ref[:, 256:512] = roped(P_DIL + 256, 1.0)
    zdil_ref[:, 512:768] = proj(P_DIL + 512, 256)
    zmla_ref[:, 0:384] = proj(P_MLA, 384)
    zmla_ref[:, 384:640] = proj(P_MLA + 384, 256)


def _out_ffn2_ple_kernel(final, x_ref, yna_ref, yswa_ref, ydil_ref, ymla_ref, p_ref, wout_ref,
                         n2_ref, wg_ref, wu_ref, wd_ref, plen_ref, plewg_ref, plewp_ref, fin_ref,
                         xo_ref, u_ref, acc_ref):
    x = x_ref[...]
    mix = _dot(yna_ref[...], wout_ref[0:256, :])
    mix += _dot(yswa_ref[...], wout_ref[256:512, :])
    mix += _dot(ydil_ref[...], wout_ref[512:768, :])
    mix += _dot(ymla_ref[...], wout_ref[768:1024, :])
    x = x + mix
    x = _swiglu_half_step(x, n2_ref, wg_ref, wu_ref, wd_ref, u_ref, acc_ref)
    gate = jax.nn.sigmoid(_dot(_rms(x, plen_ref[...]).astype(BF16), plewg_ref[...]))
    x = x + gate * _dot(p_ref[...].astype(BF16), plewp_ref[...])
    if final:
        x = _rms(x, fin_ref[...])
    xo_ref[...] = x


def _pair_attention(items):
    def stack_heads(q):
        lane = lax.broadcasted_iota(jnp.int32, q.shape, 1)
        zero = jnp.zeros_like(q)
        return jnp.concatenate([jnp.where(lane < HEAD_DIM, q, zero), jnp.where(lane >= HEAD_DIM, q, zero)], axis=0)

    def scores(st, it):
        q, k, _, bias2, _ = it
        st["s"] = _dot_t(stack_heads(q()), k()) + bias2()

    def row_max(st, it):
        m = jnp.max(st["s"], axis=-1, keepdims=True)
        st["m"] = m if it[4] is None else jnp.maximum(m, it[4])

    def probs(st, it):
        p = jnp.exp(st.pop("s") - st["m"])
        l = jnp.sum(p, axis=-1, keepdims=True)
        st["l"] = l if it[4] is None else l + jnp.exp(it[4] - st["m"])
        st["p"] = p.astype(BF16)

    def values(st, it):
        st["o"] = _dot(st.pop("p"), it[2]())

    stages = (scores, row_max, probs, values)
    states = [{} for _ in items]
    for wave in range(len(items) + len(stages) - 1):
        for s in reversed(range(len(stages))):
            i = wave - s
            if 0 <= i < len(items):
                stages[s](states[i], items[i])
    return [(st["m"], st["l"], st["o"]) for st in states]


def _merge_heads(x2, rows):
    top, bot = x2[:rows], x2[rows:]
    if top.shape[1] == 1:
        top = jnp.broadcast_to(top, (rows, LANES))
        bot = jnp.broadcast_to(bot, (rows, LANES))
    lane = lax.broadcasted_iota(jnp.int32, (rows, LANES), 1)
    return jnp.where(lane < HEAD_DIM, top, bot)


def _band_bias(offset, n_q, n_k, half_width):
    delta = offset + lax.broadcasted_iota(jnp.int32, (n_q, n_k), 1) - lax.broadcasted_iota(jnp.int32, (n_q, n_k), 0)
    bias = jnp.where(jnp.abs(delta) <= half_width, 0.0, NEG_INF).astype(F32)
    return jnp.concatenate([bias, bias], axis=0)


def _na_kernel(z_ref, bias_ref, gn_ref, o_ref):
    rows = SEQ // GRID_W

    def row_items(r):
        rs = jnp.clip(r - NA_ROWS // 2, 0, rows - NA_ROWS)
        shift = rs - r + NA_ROWS - 1
        q_rows = pl.ds(pl.multiple_of(r * GRID_W, GRID_W), GRID_W)
        k_rows = pl.ds(pl.multiple_of(rs * GRID_W, GRID_W), NA_KEYS)
        items = []
        for pr in range(2):
            tile = lambda rows_, base, pr=pr: z_ref[0, rows_, base + pr * LANES:base + (pr + 1) * LANES]
            items.append((functools.partial(tile, q_rows, 0), functools.partial(tile, k_rows, 256),
                          functools.partial(tile, k_rows, 512), lambda pr=pr: bias_ref[pr, shift], None))
        return q_rows, items

    def body(it, carry):
        loaded = [row_items(it * BAND_UNROLL + u) for u in range(BAND_UNROLL)]
        stats = _pair_attention([item for _, items in loaded for item in items])
        outs = [_merge_heads(o / l, GRID_W) for _, l, o in stats]
        ys = [_rms(jnp.concatenate(outs[2 * u:2 * u + 2], axis=1), gn_ref[...]).astype(BF16)
              for u in range(BAND_UNROLL)]
        for (q_rows, _), y in zip(loaded, ys):
            o_ref[0, q_rows, :] = y
        return carry

    lax.fori_loop(0, rows // BAND_UNROLL, body, 0)


def _swa_kernel(layer, sink_ref, z_ref, gn_ref, o_ref):
    head_row = lax.broadcasted_iota(jnp.int32, (2 * SWA_BLOCK, 1), 0) < SWA_BLOCK
    sinks = [jnp.where(head_row, sink_ref[layer, 2 * g], sink_ref[layer, 2 * g + 1]).astype(F32) for g in range(2)]

    def block_items(i):
        s0 = pl.multiple_of(i * SWA_BLOCK, SWA_BLOCK)
        ks = pl.multiple_of(jnp.clip(s0 - SWA_HALF, 0, SEQ - SWA_KEYS), SWA_BLOCK)
        q_rows, k_rows = pl.ds(s0, SWA_BLOCK), pl.ds(ks, SWA_KEYS)
        bias2 = functools.cache(functools.partial(_band_bias, ks - s0, SWA_BLOCK, SWA_KEYS, SWA_HALF))
        items = []
        for g in range(2):
            tile = lambda rows_, base, g=g: z_ref[0, rows_, base + g * LANES:base + (g + 1) * LANES]
            items.append((functools.partial(tile, q_rows, 0), functools.partial(tile, k_rows, 256),
                          functools.partial(tile, k_rows, 512), bias2, sinks[g]))
        return q_rows, items

    def body(it, carry):
        loaded = [block_items(it * BAND_UNROLL + u) for u in range(BAND_UNROLL)]
        stats = _pair_attention([item for _, items in loaded for item in items])
        outs = [_merge_heads(o / l, SWA_BLOCK) for _, l, o in stats]
        ys = [_rms(jnp.concatenate(outs[2 * u:2 * u + 2], axis=1), gn_ref[...]).astype(BF16)
              for u in range(BAND_UNROLL)]
        for (q_rows, _), y in zip(loaded, ys):
            o_ref[0, q_rows, :] = y
        return carry

    lax.fori_loop(0, SEQ // SWA_BLOCK // BAND_UNROLL, body, 0)


def _dil_kernel(q0_ref, q1_ref, k0_ref, k1_ref, v0_ref, v1_ref, gn_ref, o_ref, m_ref, l_ref, acc_ref):
    qkv_refs = ((q0_ref, k0_ref, v0_ref), (q1_ref, k1_ref, v1_ref))

    def run_pattern(first, dil, steps):
        res_len = SEQ // dil
        blocks_per_res = res_len // DIL_BLOCK
        n_keys = min(DIL_KEYS, res_len)

        def block_items(idx):
            res = idx // blocks_per_res
            q_lo = (idx % blocks_per_res) * DIL_BLOCK
            k_lo = jnp.clip(q_lo - steps, 0, res_len - n_keys)
            if dil == 1:
                q_rows = pl.ds(pl.multiple_of(q_lo, DIL_BLOCK), DIL_BLOCK)
                k_rows = pl.ds(pl.multiple_of(k_lo, DIL_STEPS), n_keys)
            else:
                q_rows = pl.ds(res + dil * q_lo, DIL_BLOCK, stride=dil)
                k_rows = pl.ds(res + dil * k_lo, n_keys, stride=dil)
            bias2 = functools.cache(functools.partial(_band_bias, k_lo - q_lo, DIL_BLOCK, n_keys, steps))
            items = []
            for pr in range(2):
                tile = lambda ref, rows_: ref[0, rows_, :].astype(BF16)
                q_ref, k_ref, v_ref = qkv_refs[pr]
                items.append((functools.partial(tile, q_ref, q_rows), functools.partial(tile, k_ref, k_rows),
                              functools.partial(tile, v_ref, k_rows), bias2, None))
            return q_rows, items

        def merge(new, pr, q_rows):
            m, l, o = (_merge_heads(t, DIL_BLOCK) for t in new)
            if first:
                return m, l, o
            m_old = m_ref[pr, q_rows, :]
            m_new = jnp.maximum(m_old, m)
            w_old = jnp.exp(m_old - m_new)
            w_blk = jnp.exp(m - m_new)
            return m_new, l_ref[pr, q_rows, :] * w_old + l * w_blk, acc_ref[pr, q_rows, :] * w_old + o * w_blk

        def body(it, carry):
            loaded = [block_items(it * BAND_UNROLL + u) for u in range(BAND_UNROLL)]
            stats = _pair_attention([item for _, items in loaded for item in items])
            merged = [merge(new, i % 2, loaded[i // 2][0]) for i, new in enumerate(stats)]
            for i, (m, l, o) in enumerate(merged):
                pr, q_rows = i % 2, loaded[i // 2][0]
                m_ref[pr, q_rows, :] = m
                l_ref[pr, q_rows, :] = l
                acc_ref[pr, q_rows, :] = o
            return carry

        lax.fori_loop(0, dil * blocks_per_res // BAND_UNROLL, body, 0)

    for p_idx, (window, dil) in enumerate(DIL_PATTERNS):
        run_pattern(p_idx == 0, dil, window // 2 // dil)

    y = jnp.concatenate([acc_ref[0] / l_ref[0], acc_ref[1] / l_ref[1]], axis=1)
    o_ref[0] = _rms(y, gn_ref[...]).astype(BF16)


def _mla_kernel(z_ref, qn_ref, wuq_ref, kvn_ref, wukv_ref, c_ref, s_ref, gn_ref, o_ref,
                q_scr, k_scr, v_scr, sa_scr, sb_scr, ma_scr, mb_scr):
    scale = (MLA_NOPE + MLA_ROPE) ** -0.5
    c_tab = c_ref[...]
    s_tab = s_ref[...]
    cqn = _rms(z_ref[0, :, 0:MLA_Q_LORA], qn_ref[...]).astype(BF16)
    for h in range(4):
        g = _dot(cqn, wuq_ref[:, h * 256:(h + 1) * 256])
        q_scr[h] = ((g[:, :LANES] * c_tab + g[:, LANES:] * s_tab) * scale).astype(BF16)
    ckvn = _rms(z_ref[0, :, MLA_Q_LORA:MLA_Q_LORA + MLA_KV_LORA], kvn_ref[...]).astype(BF16)
    k_pe = z_ref[0, :, 384:512] * c_tab + z_ref[0, :, 512:640] * s_tab
    for h in range(4):
        k_scr[h] = (_dot(ckvn, wukv_ref[:, h * LANES:(h + 1) * LANES]) + k_pe).astype(BF16)
    v_all = _dot(ckvn, wukv_ref[:, 512:768])
    lane = lax.broadcasted_iota(jnp.int32, v_all.shape, 1)
    for h in range(4):
        v_scr[h] = jnp.where((lane // HEAD_DIM) == h, v_all, 0.0).astype(BF16)

    heads = range(4)
    n_blocks = SEQ // MLA_QBLOCK

    def scores(h, blk, s_scr, m_scr):
        q0 = pl.multiple_of(blk * MLA_QBLOCK, MLA_QBLOCK)
        s = _dot_t(q_scr[h, pl.ds(q0, MLA_QBLOCK), :], k_scr[h])
        s_scr[h] = s
        m_scr[h] = jnp.broadcast_to(jnp.max(s, axis=-1, keepdims=True), (MLA_QBLOCK, LANES))

    def finish(cur, blk, nxt, nxt_blk):
        ys = []
        for h in heads:
            scores(h, nxt_blk, *nxt)
            s_scr, m_scr = cur
            p = jnp.exp(s_scr[h] - jnp.concatenate([m_scr[h]] * (SEQ // LANES), axis=1))
            l = jnp.sum(p, axis=-1, keepdims=True)
            ys.append(_dot(p.astype(BF16), v_scr[h]) / l)
        y = (ys[0] + ys[1]) + (ys[2] + ys[3])
        q0 = pl.multiple_of(blk * MLA_QBLOCK, MLA_QBLOCK)
        o_ref[0, pl.ds(q0, MLA_QBLOCK), :] = _rms(y, gn_ref[...]).astype(BF16)

    buf_a, buf_b = (sa_scr, ma_scr), (sb_scr, mb_scr)
    for h in heads:
        scores(h, 0, *buf_a)

    def two_blocks(j, carry):
        finish(buf_a, 2 * j, buf_b, 2 * j + 1)
        finish(buf_b, 2 * j + 1, buf_a, (2 * j + 2) % n_blocks)
        return carry

    lax.fori_loop(0, n_blocks // 2, two_blocks, 0)


def _rope_tables_64():
    half = HEAD_DIM // 2
    inv_freq = np.float32(ROPE_THETA) ** (-np.arange(half, dtype=np.float32) / np.float32(half))
    ang = np.arange(SEQ, dtype=np.float32)[:, None] * inv_freq[None, :]
    cos, sin = np.cos(ang).astype(np.float32), np.sin(ang).astype(np.float32)
    cos_h = np.concatenate([cos, cos], axis=1)
    sin_h = np.concatenate([-sin, sin], axis=1)
    return jnp.asarray(np.tile(cos_h, (1, 2))), jnp.asarray(np.tile(sin_h, (1, 2)))


def _rope_tables_mla():
    half = MLA_ROPE // 2
    inv_freq = np.float32(ROPE_THETA) ** (-np.arange(half, dtype=np.float32) / np.float32(half))
    ang = np.arange(SEQ, dtype=np.float32)[:, None] * inv_freq[None, :]
    cos, sin = np.cos(ang).astype(np.float32), np.sin(ang).astype(np.float32)
    ones = np.ones((SEQ, MLA_NOPE), np.float32)
    zeros_n = np.zeros((SEQ, MLA_NOPE), np.float32)
    zeros_p = np.zeros((SEQ, LANES - MLA_NOPE - MLA_ROPE), np.float32)
    c_tab = np.concatenate([ones, cos, cos, zeros_p], axis=1)
    s_tab = np.concatenate([zeros_n, -sin, sin, zeros_p], axis=1)
    return jnp.asarray(c_tab), jnp.asarray(s_tab)


def _cols(w, pieces):
    parts = [jnp.zeros(w.shape[:-1] + (pc,), w.dtype) if isinstance(pc, int) else w[..., pc[0]:pc[0] + pc[1]]
             for pc in pieces]
    return jnp.concatenate(parts, axis=-1)


def _swapped(start, width):
    return [(start + width // 2, width // 2), (start, width // 2)]


def _proj_weight(w_in):
    pad = LANES - MLA_NOPE - MLA_ROPE
    kpe = MLA_OFF + MLA_Q_LORA + MLA_KV_LORA
    k0, k1 = (SWA_OFF + 256, 64), (SWA_OFF + 320, 64)
    v0, v1 = (SWA_OFF + 384, 64), (SWA_OFF + 448, 64)
    pieces = ([(NA_OFF, 768), (SWA_OFF, 256), k0, k0, k1, k1, v0, v0, v1, v1, (DIL_OFF, 768),
               (MLA_OFF, MLA_Q_LORA + MLA_KV_LORA), MLA_NOPE, (kpe, MLA_ROPE), pad, MLA_NOPE]
              + _swapped(kpe, MLA_ROPE) + [pad])
    w = _cols(w_in, pieces).astype(BF16)
    assert w.shape[-1] == P_COLS
    return w


def _wuq_weight(w_uq):
    pad = LANES - MLA_NOPE - MLA_ROPE
    pieces = []
    for h in range(4):
        base = h * (MLA_NOPE + MLA_ROPE)
        pieces += [(base, MLA_NOPE + MLA_ROPE), pad, MLA_NOPE] + _swapped(base + MLA_NOPE, MLA_ROPE) + [pad]
    return _cols(w_uq, pieces).astype(BF16)


def _wukv_weight(w_ukv):
    pieces = []
    for h in range(4):
        pieces += [(h * 128, MLA_NOPE), LANES - MLA_NOPE]
    pieces += [(h * 128 + MLA_NOPE, HEAD_DIM) for h in range(4)]
    return _cols(w_ukv, pieces).astype(BF16)


def _na_bias_tables(bias):
    n_layers = bias.shape[0]
    qc = np.arange(GRID_W)[:, None]
    kc = np.arange(GRID_W)[None, :]
    ws = np.clip(qc - NA_COLS // 2, 0, GRID_W - NA_COLS)
    valid = jnp.asarray((kc >= ws) & (kc < ws + NA_COLS))
    lo = GRID_W - NA_COLS
    padded = jnp.pad(bias, ((0, 0), (0, 0), (0, 0), (lo, lo)))
    toep = jnp.stack([padded[..., GRID_W - 1 - c:2 * GRID_W - 1 - c] for c in range(GRID_W)], axis=-2)
    toep = jnp.where(valid, toep, NEG_INF)
    g = jnp.stack([toep[:, :, s:s + NA_ROWS] for s in range(NA_ROWS)], axis=2)
    g = g.transpose(0, 1, 2, 4, 3, 5).reshape(n_layers, 2, 2, NA_ROWS, GRID_W, NA_KEYS)
    return g.transpose(0, 1, 3, 2, 4, 5).reshape(n_layers, 2, NA_ROWS, 2 * GRID_W, NA_KEYS)


def _tok_spec(width):
    return pl.BlockSpec((TOKEN_TILE, width), lambda i: (i, 0))


def _ffn1_proj(x2, params, cos, sin):
    n_tok = x2.shape[0]
    tiles_per_seq = SEQ // TOKEN_TILE
    pos_spec = pl.BlockSpec((TOKEN_TILE, LANES), lambda i: (i % tiles_per_seq, 0))
    return pl.pallas_call(
        _ffn1_proj_kernel,
        grid=(n_tok // TOKEN_TILE,),
        in_specs=[_tok_spec(D_MODEL)] + [_resident(prm) for prm in params] + [pos_spec, pos_spec],
        out_specs=[_tok_spec(D_MODEL), _tok_spec(768), _tok_spec(768), _tok_spec(768), _tok_spec(MLA_COLS)],
        out_shape=[jax.ShapeDtypeStruct((n_tok, D_MODEL), F32),
                   jax.ShapeDtypeStruct((n_tok, 768), BF16),
                   jax.ShapeDtypeStruct((n_tok, 768), BF16),
                   jax.ShapeDtypeStruct((n_tok, 768), F32),
                   jax.ShapeDtypeStruct((n_tok, MLA_COLS), F32)],
        scratch_shapes=[pltpu.VMEM((TOKEN_TILE, D_MODEL), BF16), pltpu.VMEM((TOKEN_TILE, D_MODEL), F32)],
        compiler_params=_cparams(),
        name="ffn1_proj",
    )(x2, *[arr for arr, _ in params], cos, sin)


def _out_ffn2_ple(final, layer, x2, ys, p3, params):
    n_tok = x2.shape[0]
    p_spec = pl.BlockSpec((None, TOKEN_TILE, PLE_DIM), lambda i: (layer, i, 0))
    return pl.pallas_call(
        functools.partial(_out_ffn2_ple_kernel, final),
        grid=(n_tok // TOKEN_TILE,),
        in_specs=[_tok_spec(D_MODEL)] + [_tok_spec(GROUP_WIDTH)] * 4 + [p_spec]
        + [_resident(prm) for prm in params],
        out_specs=_tok_spec(D_MODEL),
        out_shape=jax.ShapeDtypeStruct((n_tok, D_MODEL), F32),
        scratch_shapes=[pltpu.VMEM((TOKEN_TILE, D_MODEL), BF16), pltpu.VMEM((TOKEN_TILE, D_MODEL), F32)],
        compiler_params=_cparams(),
        name="out_ffn2_ple",
    )(x2, *ys, p3, *[arr for arr, _ in params])


def _seq_spec(width):
    return pl.BlockSpec((1, SEQ, width), lambda b: (b, 0, 0))


def _y_shape(batch):
    return jax.ShapeDtypeStruct((batch, SEQ, GROUP_WIDTH), BF16)


def _na_mixer(z, bias_tab, gn):
    batch = z.shape[0]
    return pl.pallas_call(
        _na_kernel, grid=(batch,),
        in_specs=[_seq_spec(768), _resident(bias_tab), _resident(gn)],
        out_specs=_seq_spec(GROUP_WIDTH), out_shape=_y_shape(batch),
        compiler_params=_cparams(), name="na_mixer",
    )(z, bias_tab[0], gn[0])


def _swa_mixer(layer, z, sinks, gn):
    batch = z.shape[0]
    return pl.pallas_call(
        functools.partial(_swa_kernel, layer), grid=(batch,),
        in_specs=[pl.BlockSpec(memory_space=pltpu.SMEM), _seq_spec(768), _resident(gn)],
        out_specs=_seq_spec(GROUP_WIDTH), out_shape=_y_shape(batch),
        compiler_params=_cparams(), name="swa_mixer",
    )(sinks, z, gn[0])


def _dil_mixer(z, gn):
    batch = z.shape[0]
    pair_specs = [pl.BlockSpec((1, SEQ, LANES), functools.partial(lambda c, b: (b, 0, c), c)) for c in range(6)]
    return pl.pallas_call(
        _dil_kernel, grid=(batch,),
        in_specs=pair_specs + [_resident(gn)],
        out_specs=_seq_spec(GROUP_WIDTH), out_shape=_y_shape(batch),
        scratch_shapes=[pltpu.VMEM((2, SEQ, LANES), F32)] * 3,
        compiler_params=_cparams(), name="dil_mixer",
    )(z, z, z, z, z, z, gn[0])


def _mla_mixer(z, params):
    batch = z.shape[0]
    return pl.pallas_call(
        _mla_kernel, grid=(batch,),
        in_specs=[_seq_spec(MLA_COLS)] + [_resident(prm) for prm in params],
        out_specs=_seq_spec(GROUP_WIDTH), out_shape=_y_shape(batch),
        scratch_shapes=[pltpu.VMEM((4, SEQ, LANES), BF16), pltpu.VMEM((4, SEQ, LANES), BF16),
                        pltpu.VMEM((4, SEQ, GROUP_WIDTH), BF16),
                        pltpu.VMEM((4, MLA_QBLOCK, SEQ), F32), pltpu.VMEM((4, MLA_QBLOCK, SEQ), F32),
                        pltpu.VMEM((4, MLA_QBLOCK, LANES), F32), pltpu.VMEM((4, MLA_QBLOCK, LANES), F32)],
        compiler_params=_cparams(), name="mla_mixer",
    )(z, *[arr for arr, _ in params])


def kernel(x, p, ffn1_norm, ffn1_w_gate, ffn1_w_up, ffn1_w_down, mix_norm, w_in, na_bias, swa_sink, mla_q_norm, mla_w_uq, mla_kv_norm, mla_w_ukv, group_norm, w_out, ffn2_norm, ffn2_w_gate, ffn2_w_up, ffn2_w_down, ple_norm, ple_w_gate, ple_w_proj, final_norm):
    batch, seq, d_model = x.shape
    assert (seq, d_model) == (SEQ, D_MODEL) and x.dtype == F32
    n_tok = batch * seq
    cos64, sin64 = _rope_tables_64()
    c_mla, s_mla = _rope_tables_mla()
    rows = lambda v: v.reshape(v.shape[:-1] + (1, v.shape[-1]))
    bf = lambda w: w.astype(BF16)
    ffn1 = [rows(ffn1_norm), bf(ffn1_w_gate), bf(ffn1_w_up), bf(ffn1_w_down), rows(mix_norm), _proj_weight(w_in)]
    ffn2 = [bf(w_out), rows(ffn2_norm), bf(ffn2_w_gate), bf(ffn2_w_up), bf(ffn2_w_down), rows(ple_norm),
            bf(ple_w_gate), bf(ple_w_proj)]
    mla = [rows(mla_q_norm), _wuq_weight(mla_w_uq), rows(mla_kv_norm), _wukv_weight(mla_w_ukv)]
    na_tables = _na_bias_tables(na_bias)
    gains = rows(group_norm)
    p3 = p.reshape(DEPTH, n_tok, PLE_DIM)

    x2 = x.reshape(n_tok, d_model)
    for i in range(DEPTH):
        at = lambda arrs: [(a, (i,)) for a in arrs]
        gain = lambda g: (gains, (i, g))
        x2, z_na, z_swa, z_dil, z_mla = _ffn1_proj(x2, at(ffn1), cos64, sin64)
        seq_view = lambda z: z.reshape(batch, seq, z.shape[-1])
        y_na = _na_mixer(seq_view(z_na), (na_tables, (i,)), gain(0))
        y_swa = _swa_mixer(i, seq_view(z_swa), swa_sink, gain(1))
        y_dil = _dil_mixer(seq_view(z_dil), gain(2))
        y_mla = _mla_mixer(seq_view(z_mla), at(mla) + [(c_mla, ()), (s_mla, ()), gain(3)])
        ys = [y.reshape(n_tok, GROUP_WIDTH) for y in (y_na, y_swa, y_dil, y_mla)]
        x2 = _out_ffn2_ple(i == DEPTH - 1, i, x2, ys, p3, at(ffn2) + [(final_norm.reshape(1, -1), ())])
    return x2.reshape(batch, seq, d_model)
```

```python
import functools
import math

import numpy as np
import jax
import jax.numpy as jnp
from jax import lax
from jax.experimental import pallas as pl
from jax.experimental.pallas import tpu as pltpu

F32 = jnp.float32
BF16 = jnp.bfloat16

LANES = 128
VMEM_LIMIT_BYTES = 58 * 1024 * 1024

D_MODEL = 1024
SEQ = 2048
DEPTH = 4
HEAD_DIM = 64
GROUP_WIDTH = 256
D_FF = 2816
FF_CHUNK = 256
N_FF_CHUNKS = D_FF // FF_CHUNK
PLE_DIM = 256
ROPE_THETA = 10000.0
EPS = 1e-6
NEG_INF = -1e30
LOG2E = math.log2(math.e)

GRID_W = 64
NA_ROWS = 8
NA_COLS = 16
NA_KEYS = NA_ROWS * GRID_W
SWA_HALF = 128
SWA_BLOCK = 128
SWA_KEYS = SWA_BLOCK + 2 * SWA_HALF
DIL_PATTERNS = ((128, 1), (512, 4), (2048, 16))
DIL_STEPS = 64
DIL_BLOCK = 128
DIL_KEYS = DIL_BLOCK + 2 * DIL_STEPS
MLA_Q_LORA = 256
MLA_KV_LORA = 128
MLA_NOPE = 64
MLA_ROPE = 32
MLA_QBLOCK = 256
MLA_VROWS = HEAD_DIM + 16

NA_OFF, SWA_OFF, DIL_OFF, MLA_OFF = 0, 768, 1280, 2048
P_NA, P_SWA, P_DIL, P_MLA = 0, 768, 1536, 2304
P_COLS = 2944
MLA_COLS = 640

TOKEN_TILE = 512
BAND_UNROLL = 4


def _cparams(n_axes=1):
    return pltpu.CompilerParams(dimension_semantics=("arbitrary",) * n_axes,
                                vmem_limit_bytes=VMEM_LIMIT_BYTES)


def _resident(param):
    arr, lead = param
    rest = arr.shape[len(lead):]
    return pl.BlockSpec((None,) * len(lead) + rest, lambda *_: lead + (0,) * len(rest),
                        pipeline_mode=pl.Buffered(1))


def _rms(x, g):
    return x * lax.rsqrt(jnp.mean(x * x, axis=-1, keepdims=True) + EPS) * g


def _dot(a, b):
    return jnp.dot(a, b, preferred_element_type=F32)


def _dot_t(a, b):
    return lax.dot_general(a, b, (((1,), (1,)), ((), ())), preferred_element_type=F32)


def _swiglu_half_step(x, norm_ref, wg_ref, wu_ref, wd_ref, u_ref, acc_ref):
    u_ref[...] = _rms(x, norm_ref[...]).astype(BF16)
    acc_ref[...] = jnp.zeros_like(acc_ref)

    for j in range(N_FF_CHUNKS):
        cols = slice(j * FF_CHUNK, (j + 1) * FF_CHUNK)
        u = u_ref[...]
        g = _dot(u, wg_ref[:, cols])
        up = _dot(u, wu_ref[:, cols])
        a = (g * jax.nn.sigmoid(g) * up).astype(BF16)
        acc_ref[...] += _dot(a, wd_ref[cols, :])
    return x + 0.5 * acc_ref[...]


def _rope_tile(x, cos, sin_signed):
    lane = lax.broadcasted_iota(jnp.int32, x.shape, 1)
    first_half = (lane % HEAD_DIM) < (HEAD_DIM // 2)
    partner = jnp.where(first_half, pltpu.roll(x, LANES - HEAD_DIM // 2, 1), pltpu.roll(x, HEAD_DIM // 2, 1))
    return x * cos + partner * sin_signed


def _ffn1_proj_kernel(x_ref, n1_ref, wg_ref, wu_ref, wd_ref, mixn_ref, win_ref, cos_ref, sin_ref,
                      xo_ref, zna_ref, zswa_ref, zdil_ref, zmla_ref, u_ref, acc_ref):
    x = _swiglu_half_step(x_ref[...], n1_ref, wg_ref, wu_ref, wd_ref, u_ref, acc_ref)
    xo_ref[...] = x
    u_ref[...] = _rms(x, mixn_ref[...]).astype(BF16)
    cos = cos_ref[...]
    sin = sin_ref[...]
    q_scale = HEAD_DIM ** -0.5 * LOG2E

    def proj(col, width):
        return _dot(u_ref[...], win_ref[:, col:col + width])

    def roped(col, scale):
        z = proj(col, 2 * LANES)
        halves = [_rope_tile(z[:, h * LANES:(h + 1) * LANES], cos, sin) for h in range(2)]
        return jnp.concatenate(halves, axis=1) * scale

    zna_ref[:, 0:256] = (proj(P_NA, 256) * q_scale).astype(BF16)
    zna_ref[:, 256:768] = proj(P_NA + 256, 512).astype(BF16)
    zswa_ref[:, 0:256] = roped(P_SWA, q_scale).astype(BF16)
    zswa_ref[:, 256:512] = roped(P_SWA + 256, 1.0).astype(BF16)
    zswa_ref[:, 512:768] = proj(P_SWA + 512, 256).astype(BF16)
    zdil_ref[:, 0:256] = roped(P_DIL, q_scale)
    zdil_ref[:, 256:512] = roped(P_DIL + 256, 1.0)
    zdil_ref[:, 512:768] = proj(P_DIL + 512, 256)
    zmla_ref[:, 0:384] = proj(P_MLA, 384)
    zmla_ref[:, 384:640] = proj(P_MLA + 384, 256)


def _out_ffn2_ple_kernel(final, x_ref, yna_ref, yswa_ref, ydil_ref, ymla_ref, p_ref, wout_ref,
                         n2_ref, wg_ref, wu_ref, wd_ref, plen_ref, plewg_ref, plewp_ref, fin_ref,
                         xo_ref, u_ref, acc_ref):
    x = x_ref[...]
    mix = _dot(yna_ref[...], wout_ref[0:256, :])
    mix += _dot(yswa_ref[...], wout_ref[256:512, :])
    mix += _dot(ydil_ref[...], wout_ref[512:768, :])
    mix += _dot(ymla_ref[...], wout_ref[768:1024, :])
    x = x + mix
    x = _swiglu_half_step(x, n2_ref, wg_ref, wu_ref, wd_ref, u_ref, acc_ref)
    gate = jax.nn.sigmoid(_dot(_rms(x, plen_ref[...]).astype(BF16), plewg_ref[...]))
    x = x + gate * _dot(p_ref[...].astype(BF16), plewp_ref[...])
    if final:
        x = _rms(x, fin_ref[...])
    xo_ref[...] = x


def _pair_attention(items):
    def stack_heads(q):
        lane = lax.broadcasted_iota(jnp.int32, q.shape, 1)
        zero = jnp.zeros_like(q)
        return jnp.concatenate([jnp.where(lane < HEAD_DIM, q, zero), jnp.where(lane >= HEAD_DIM, q, zero)], axis=0)

    def scores(st, it):
        q, k, _, bias2, _ = it
        st["s"] = _dot_t(stack_heads(q()), k()) + bias2()

    def row_max(st, it):
        m = jnp.max(st["s"], axis=-1, keepdims=True)
        st["m"] = m if it[4] is None else jnp.maximum(m, it[4])

    def probs(st, it):
        p = jnp.exp2(st.pop("s") - st["m"])
        l = jnp.sum(p, axis=-1, keepdims=True)
        st["l"] = l if it[4] is None else l + jnp.exp2(it[4] - st["m"])
        st["p"] = p.astype(BF16)

    def values(st, it):
        st["o"] = _dot(st.pop("p"), it[2]())

    stages = (scores, row_max, probs, values)
    states = [{} for _ in items]
    for wave in range(len(items) + len(stages) - 1):
        for s in reversed(range(len(stages))):
            i = wave - s
            if 0 <= i < len(items):
                stages[s](states[i], items[i])
    return [(st["m"], st["l"], st["o"]) for st in states]


def _merge_heads(x2, rows):
    top, bot = x2[:rows], x2[rows:]
    if top.shape[1] == 1:
        top = jnp.broadcast_to(top, (rows, LANES))
        bot = jnp.broadcast_to(bot, (rows, LANES))
    lane = lax.broadcasted_iota(jnp.int32, (rows, LANES), 1)
    return jnp.where(lane < HEAD_DIM, top, bot)


def _band_bias(offset, n_q, n_k, half_width):
    delta = offset + lax.broadcasted_iota(jnp.int32, (n_q, n_k), 1) - lax.broadcasted_iota(jnp.int32, (n_q, n_k), 0)
    bias = jnp.where(jnp.abs(delta) <= half_width, 0.0, NEG_INF).astype(F32)
    return jnp.concatenate([bias, bias], axis=0)


def _na_kernel(z_ref, bias_ref, gn_ref, o_ref):
    rows = SEQ // GRID_W

    def row_items(r):
        rs = jnp.clip(r - NA_ROWS // 2, 0, rows - NA_ROWS)
        shift = rs - r + NA_ROWS - 1
        q_rows = pl.ds(pl.multiple_of(r * GRID_W, GRID_W), GRID_W)
        k_rows = pl.ds(pl.multiple_of(rs * GRID_W, GRID_W), NA_KEYS)
        items = []
        for pr in range(2):
            tile = lambda rows_, base, pr=pr: z_ref[0, rows_, base + pr * LANES:base + (pr + 1) * LANES]
            items.append((functools.partial(tile, q_rows, 0), functools.partial(tile, k_rows, 256),
                          functools.partial(tile, k_rows, 512), lambda pr=pr: bias_ref[pr, shift], None))
        return q_rows, items

    def body(it, carry):
        loaded = [row_items(it * BAND_UNROLL + u) for u in range(BAND_UNROLL)]
        stats = _pair_attention([item for _, items in loaded for item in items])
        outs = [_merge_heads(o / l, GRID_W) for _, l, o in stats]
        ys = [_rms(jnp.concatenate(outs[2 * u:2 * u + 2], axis=1), gn_ref[...]).astype(BF16)
              for u in range(BAND_UNROLL)]
        for (q_rows, _), y in zip(loaded, ys):
            o_ref[0, q_rows, :] = y
        return carry

    lax.fori_loop(0, rows // BAND_UNROLL, body, 0)


def _swa_kernel(layer, sink_ref, z_ref, gn_ref, o_ref):
    head_row = lax.broadcasted_iota(jnp.int32, (2 * SWA_BLOCK, 1), 0) < SWA_BLOCK
    sinks = [jnp.where(head_row, sink_ref[layer, 2 * g], sink_ref[layer, 2 * g + 1]).astype(F32) * LOG2E
             for g in range(2)]

    def block_items(i):
        s0 = pl.multiple_of(i * SWA_BLOCK, SWA_BLOCK)
        ks = pl.multiple_of(jnp.clip(s0 - SWA_HALF, 0, SEQ - SWA_KEYS), SWA_BLOCK)
        q_rows, k_rows = pl.ds(s0, SWA_BLOCK), pl.ds(ks, SWA_KEYS)
        bias2 = functools.cache(functools.partial(_band_bias, ks - s0, SWA_BLOCK, SWA_KEYS, SWA_HALF))
        items = []
        for g in range(2):
            tile = lambda rows_, base, g=g: z_ref[0, rows_, base + g * LANES:base + (g + 1) * LANES]
            items.append((functools.partial(tile, q_rows, 0), functools.partial(tile, k_rows, 256),
                          functools.partial(tile, k_rows, 512), bias2, sinks[g]))
        return q_rows, items

    def body(it, carry):
        loaded = [block_items(it * BAND_UNROLL + u) for u in range(BAND_UNROLL)]
        stats = _pair_attention([item for _, items in loaded for item in items])
        outs = [_merge_heads(o / l, SWA_BLOCK) for _, l, o in stats]
        ys = [_rms(jnp.concatenate(outs[2 * u:2 * u + 2], axis=1), gn_ref[...]).astype(BF16)
              for u in range(BAND_UNROLL)]
        for (q_rows, _), y in zip(loaded, ys):
            o_ref[0, q_rows, :] = y
        return carry

    lax.fori_loop(0, SEQ // SWA_BLOCK // BAND_UNROLL, body, 0)


def _dil_kernel(q0_ref, q1_ref, k0_ref, k1_ref, v0_ref, v1_ref, gn_ref, o_ref, m_ref, l_ref, acc_ref):
    qkv_refs = ((q0_ref, k0_ref, v0_ref), (q1_ref, k1_ref, v1_ref))

    def run_pattern(first, dil, steps):
        res_len = SEQ // dil
        blocks_per_res = res_len // DIL_BLOCK
        n_keys = min(DIL_KEYS, res_len)

        def block_items(idx):
            res = idx // blocks_per_res
            q_lo = (idx % blocks_per_res) * DIL_BLOCK
            k_lo = jnp.clip(q_lo - steps, 0, res_len - n_keys)
            if dil == 1:
                q_rows = pl.ds(pl.multiple_of(q_lo, DIL_BLOCK), DIL_BLOCK)
                k_rows = pl.ds(pl.multiple_of(k_lo, DIL_STEPS), n_keys)
            else:
                q_rows = pl.ds(res + dil * q_lo, DIL_BLOCK, stride=dil)
                k_rows = pl.ds(res + dil * k_lo, n_keys, stride=dil)
            bias2 = functools.cache(functools.partial(_band_bias, k_lo - q_lo, DIL_BLOCK, n_keys, steps))
            items = []
            for pr in range(2):
                tile = lambda ref, rows_: ref[0, rows_, :].astype(BF16)
                q_ref, k_ref, v_ref = qkv_refs[pr]
                items.append((functools.partial(tile, q_ref, q_rows), functools.partial(tile, k_ref, k_rows),
                              functools.partial(tile, v_ref, k_rows), bias2, None))
            return q_rows, items

        def merge(new, pr, q_rows):
            m, l, o = (_merge_heads(t, DIL_BLOCK) for t in new)
            if first:
                return m, l, o
            m_old = m_ref[pr, q_rows, :]
            m_new = jnp.maximum(m_old, m)
            w_old = jnp.exp2(m_old - m_new)
            w_blk = jnp.exp2(m - m_new)
            return m_new, l_ref[pr, q_rows, :] * w_old + l * w_blk, acc_ref[pr, q_rows, :] * w_old + o * w_blk

        def body(it, carry):
            loaded = [block_items(it * BAND_UNROLL + u) for u in range(BAND_UNROLL)]
            stats = _pair_attention([item for _, items in loaded for item in items])
            merged = [merge(new, i % 2, loaded[i // 2][0]) for i, new in enumerate(stats)]
            for i, (m, l, o) in enumerate(merged):
                pr, q_rows = i % 2, loaded[i // 2][0]
                m_ref[pr, q_rows, :] = m
                l_ref[pr, q_rows, :] = l
                acc_ref[pr, q_rows, :] = o
            return carry

        lax.fori_loop(0, dil * blocks_per_res // BAND_UNROLL, body, 0)

    for p_idx, (window, dil) in enumerate(DIL_PATTERNS):
        run_pattern(p_idx == 0, dil, window // 2 // dil)

    y = jnp.concatenate([acc_ref[0] / l_ref[0], acc_ref[1] / l_ref[1]], axis=1)
    o_ref[0] = _rms(y, gn_ref[...]).astype(BF16)


def _mla_kernel(z_ref, qn_ref, wuq_ref, kvn_ref, wukv_ref, c_ref, s_ref, gn_ref, o_ref,
                q_scr, k_scr, vt_scr, sa_scr, sb_scr, ma_scr, mb_scr):
    scale = (MLA_NOPE + MLA_ROPE) ** -0.5 * LOG2E
    c_tab = c_ref[...]
    s_tab = s_ref[...]
    cqn = _rms(z_ref[0, :, 0:MLA_Q_LORA], qn_ref[...]).astype(BF16)
    for h in range(4):
        g = _dot(cqn, wuq_ref[:, h * 256:(h + 1) * 256])
        q_scr[h] = ((g[:, :LANES] * c_tab + g[:, LANES:] * s_tab) * scale).astype(BF16)
    ckvn = _rms(z_ref[0, :, MLA_Q_LORA:MLA_Q_LORA + MLA_KV_LORA], kvn_ref[...]).astype(BF16)
    k_pe = z_ref[0, :, 384:512] * c_tab + z_ref[0, :, 512:640] * s_tab
    for h in range(4):
        k_scr[h] = (_dot(ckvn, wukv_ref[:, h * LANES:(h + 1) * LANES]) + k_pe).astype(BF16)
    vt = _dot(ckvn, wukv_ref[:, 512:768]).T
    pad_row = lax.broadcasted_iota(jnp.int32, (MLA_VROWS - HEAD_DIM, SEQ), 0)
    ones_then_zeros = jnp.where(pad_row == 0, 1.0, 0.0).astype(F32)
    for h in range(4):
        vt_scr[h] = jnp.concatenate([vt[h * HEAD_DIM:(h + 1) * HEAD_DIM], ones_then_zeros], axis=0).astype(BF16)

    heads = range(4)
    n_blocks = SEQ // MLA_QBLOCK

    def scores(h, blk, s_scr, m_scr):
        q0 = pl.multiple_of(blk * MLA_QBLOCK, MLA_QBLOCK)
        s = _dot_t(k_scr[h], q_scr[h, pl.ds(q0, MLA_QBLOCK), :])
        s_scr[h] = s
        m_scr[h] = jnp.broadcast_to(jnp.max(s, axis=0, keepdims=True), (8, MLA_QBLOCK))

    def finish(cur, blk, nxt, nxt_blk):
        yts = []
        for h in heads:
            scores(h, nxt_blk, *nxt)
            s_scr, m_scr = cur
            p = jnp.exp2(s_scr[h] - m_scr[h][0:1, :])
            ol = _dot(vt_scr[h], p.astype(BF16))
            yts.append(ol[:HEAD_DIM] / ol[HEAD_DIM:HEAD_DIM + 1])
        y = jnp.concatenate(yts, axis=0).T
        q0 = pl.multiple_of(blk * MLA_QBLOCK, MLA_QBLOCK)
        o_ref[0, pl.ds(q0, MLA_QBLOCK), :] = _rms(y, gn_ref[...]).astype(BF16)

    buf_a, buf_b = (sa_scr, ma_scr), (sb_scr, mb_scr)
    for h in heads:
        scores(h, 0, *buf_a)

    def two_blocks(j, carry):
        finish(buf_a, 2 * j, buf_b, 2 * j + 1)
        finish(buf_b, 2 * j + 1, buf_a, (2 * j + 2) % n_blocks)
        return carry

    lax.fori_loop(0, n_blocks // 2, two_blocks, 0)


def _rope_tables_64():
    half = HEAD_DIM // 2
    inv_freq = np.float32(ROPE_THETA) ** (-np.arange(half, dtype=np.float32) / np.float32(half))
    ang = np.arange(SEQ, dtype=np.float32)[:, None] * inv_freq[None, :]
    cos, sin = np.cos(ang).astype(np.float32), np.sin(ang).astype(np.float32)
    cos_h = np.concatenate([cos, cos], axis=1)
    sin_h = np.concatenate([-sin, sin], axis=1)
    return jnp.asarray(np.tile(cos_h, (1, 2))), jnp.asarray(np.tile(sin_h, (1, 2)))


def _rope_tables_mla():
    half = MLA_ROPE // 2
    inv_freq = np.float32(ROPE_THETA) ** (-np.arange(half, dtype=np.float32) / np.float32(half))
    ang = np.arange(SEQ, dtype=np.float32)[:, None] * inv_freq[None, :]
    cos, sin = np.cos(ang).astype(np.float32), np.sin(ang).astype(np.float32)
    ones = np.ones((SEQ, MLA_NOPE), np.float32)
    zeros_n = np.zeros((SEQ, MLA_NOPE), np.float32)
    zeros_p = np.zeros((SEQ, LANES - MLA_NOPE - MLA_ROPE), np.float32)
    c_tab = np.concatenate([ones, cos, cos, zeros_p], axis=1)
    s_tab = np.concatenate([zeros_n, -sin, sin, zeros_p], axis=1)
    return jnp.asarray(c_tab), jnp.asarray(s_tab)


def _cols(w, pieces):
    parts = [jnp.zeros(w.shape[:-1] + (pc,), w.dtype) if isinstance(pc, int) else w[..., pc[0]:pc[0] + pc[1]]
             for pc in pieces]
    return jnp.concatenate(parts, axis=-1)


def _swapped(start, width):
    return [(start + width // 2, width // 2), (start, width // 2)]


def _proj_weight(w_in):
    pad = LANES - MLA_NOPE - MLA_ROPE
    kpe = MLA_OFF + MLA_Q_LORA + MLA_KV_LORA
    k0, k1 = (SWA_OFF + 256, 64), (SWA_OFF + 320, 64)
    v0, v1 = (SWA_OFF + 384, 64), (SWA_OFF + 448, 64)
    pieces = ([(NA_OFF, 768), (SWA_OFF, 256), k0, k0, k1, k1, v0, v0, v1, v1, (DIL_OFF, 768),
               (MLA_OFF, MLA_Q_LORA + MLA_KV_LORA), MLA_NOPE, (kpe, MLA_ROPE), pad, MLA_NOPE]
              + _swapped(kpe, MLA_ROPE) + [pad])
    w = _cols(w_in, pieces).astype(BF16)
    assert w.shape[-1] == P_COLS
    return w


def _wuq_weight(w_uq):
    pad = LANES - MLA_NOPE - MLA_ROPE
    pieces = []
    for h in range(4):
        base = h * (MLA_NOPE + MLA_ROPE)
        pieces += [(base, MLA_NOPE + MLA_ROPE), pad, MLA_NOPE] + _swapped(base + MLA_NOPE, MLA_ROPE) + [pad]
    return _cols(w_uq, pieces).astype(BF16)


def _wukv_weight(w_ukv):
    pieces = []
    for h in range(4):
        pieces += [(h * 128, MLA_NOPE), LANES - MLA_NOPE]
    pieces += [(h * 128 + MLA_NOPE, HEAD_DIM) for h in range(4)]
    return _cols(w_ukv, pieces).astype(BF16)


def _na_bias_tables(bias):
    n_layers = bias.shape[0]
    qc = np.arange(GRID_W)[:, None]
    kc = np.arange(GRID_W)[None, :]
    ws = np.clip(qc - NA_COLS // 2, 0, GRID_W - NA_COLS)
    valid = jnp.asarray((kc >= ws) & (kc < ws + NA_COLS))
    lo = GRID_W - NA_COLS
    padded = jnp.pad(bias, ((0, 0), (0, 0), (0, 0), (lo, lo)))
    toep = jnp.stack([padded[..., GRID_W - 1 - c:2 * GRID_W - 1 - c] for c in range(GRID_W)], axis=-2)
    toep = jnp.where(valid, toep * LOG2E, NEG_INF)
    g = jnp.stack([toep[:, :, s:s + NA_ROWS] for s in range(NA_ROWS)], axis=2)
    g = g.transpose(0, 1, 2, 4, 3, 5).reshape(n_layers, 2, 2, NA_ROWS, GRID_W, NA_KEYS)
    return g.transpose(0, 1, 3, 2, 4, 5).reshape(n_layers, 2, NA_ROWS, 2 * GRID_W, NA_KEYS)


def _tok_spec(width):
    return pl.BlockSpec((TOKEN_TILE, width), lambda i: (i, 0))


def _ffn1_proj(x2, params, cos, sin):
    n_tok = x2.shape[0]
    tiles_per_seq = SEQ // TOKEN_TILE
    pos_spec = pl.BlockSpec((TOKEN_TILE, LANES), lambda i: (i % tiles_per_seq, 0))
    return pl.pallas_call(
        _ffn1_proj_kernel,
        grid=(n_tok // TOKEN_TILE,),
        in_specs=[_tok_spec(D_MODEL)] + [_resident(prm) for prm in params] + [pos_spec, pos_spec],
        out_specs=[_tok_spec(D_MODEL), _tok_spec(768), _tok_spec(768), _tok_spec(768), _tok_spec(MLA_COLS)],
        out_shape=[jax.ShapeDtypeStruct((n_tok, D_MODEL), F32),
                   jax.ShapeDtypeStruct((n_tok, 768), BF16),
                   jax.ShapeDtypeStruct((n_tok, 768), BF16),
                   jax.ShapeDtypeStruct((n_tok, 768), F32),
                   jax.ShapeDtypeStruct((n_tok, MLA_COLS), F32)],
        scratch_shapes=[pltpu.VMEM((TOKEN_TILE, D_MODEL), BF16), pltpu.VMEM((TOKEN_TILE, D_MODEL), F32)],
        compiler_params=_cparams(),
        name="ffn1_proj",
    )(x2, *[arr for arr, _ in params], cos, sin)


def _out_ffn2_ple(final, layer, x2, ys, p3, params):
    n_tok = x2.shape[0]
    p_spec = pl.BlockSpec((None, TOKEN_TILE, PLE_DIM), lambda i: (layer, i, 0))
    return pl.pallas_call(
        functools.partial(_out_ffn2_ple_kernel, final),
        grid=(n_tok // TOKEN_TILE,),
        in_specs=[_tok_spec(D_MODEL)] + [_tok_spec(GROUP_WIDTH)] * 4 + [p_spec]
        + [_resident(prm) for prm in params],
        out_specs=_tok_spec(D_MODEL),
        out_shape=jax.ShapeDtypeStruct((n_tok, D_MODEL), F32),
        scratch_shapes=[pltpu.VMEM((TOKEN_TILE, D_MODEL), BF16), pltpu.VMEM((TOKEN_TILE, D_MODEL), F32)],
        compiler_params=_cparams(),
        name="out_ffn2_ple",
    )(x2, *ys, p3, *[arr for arr, _ in params])


def _seq_spec(width):
    return pl.BlockSpec((1, SEQ, width), lambda b: (b, 0, 0))


def _y_shape(batch):
    return jax.ShapeDtypeStruct((batch, SEQ, GROUP_WIDTH), BF16)


def _na_mixer(z, bias_tab, gn):
    batch = z.shape[0]
    return pl.pallas_call(
        _na_kernel, grid=(batch,),
        in_specs=[_seq_spec(768), _resident(bias_tab), _resident(gn)],
        out_specs=_seq_spec(GROUP_WIDTH), out_shape=_y_shape(batch),
        compiler_params=_cparams(), name="na_mixer",
    )(z, bias_tab[0], gn[0])


def _swa_mixer(layer, z, sinks, gn):
    batch = z.shape[0]
    return pl.pallas_call(
        functools.partial(_swa_kernel, layer), grid=(batch,),
        in_specs=[pl.BlockSpec(memory_space=pltpu.SMEM), _seq_spec(768), _resident(gn)],
        out_specs=_seq_spec(GROUP_WIDTH), out_shape=_y_shape(batch),
        compiler_params=_cparams(), name="swa_mixer",
    )(sinks, z, gn[0])


def _dil_mixer(z, gn):
    batch = z.shape[0]
    pair_specs = [pl.BlockSpec((1, SEQ, LANES), functools.partial(lambda c, b: (b, 0, c), c)) for c in range(6)]
    return pl.pallas_call(
        _dil_kernel, grid=(batch,),
        in_specs=pair_specs + [_resident(gn)],
        out_specs=_seq_spec(GROUP_WIDTH), out_shape=_y_shape(batch),
        scratch_shapes=[pltpu.VMEM((2, SEQ, LANES), F32)] * 3,
        compiler_params=_cparams(), name="dil_mixer",
    )(z, z, z, z, z, z, gn[0])


def _mla_mixer(z, params):
    batch = z.shape[0]
    return pl.pallas_call(
        _mla_kernel, grid=(batch,),
        in_specs=[_seq_spec(MLA_COLS)] + [_resident(prm) for prm in params],
        out_specs=_seq_spec(GROUP_WIDTH), out_shape=_y_shape(batch),
        scratch_shapes=[pltpu.VMEM((4, SEQ, LANES), BF16), pltpu.VMEM((4, SEQ, LANES), BF16),
                        pltpu.VMEM((4, MLA_VROWS, SEQ), BF16),
                        pltpu.VMEM((4, SEQ, MLA_QBLOCK), F32), pltpu.VMEM((4, SEQ, MLA_QBLOCK), F32),
                        pltpu.VMEM((4, 8, MLA_QBLOCK), F32), pltpu.VMEM((4, 8, MLA_QBLOCK), F32)],
        compiler_params=_cparams(), name="mla_mixer",
    )(z, *[arr for arr, _ in params])


def kernel(x, p, ffn1_norm, ffn1_w_gate, ffn1_w_up, ffn1_w_down, mix_norm, w_in, na_bias, swa_sink, mla_q_norm, mla_w_uq, mla_kv_norm, mla_w_ukv, group_norm, w_out, ffn2_norm, ffn2_w_gate, ffn2_w_up, ffn2_w_down, ple_norm, ple_w_gate, ple_w_proj, final_norm):
    batch, seq, d_model = x.shape
    assert (seq, d_model) == (SEQ, D_MODEL) and x.dtype == F32
    n_tok = batch * seq
    cos64, sin64 = _rope_tables_64()
    c_mla, s_mla = _rope_tables_mla()
    rows = lambda v: v.reshape(v.shape[:-1] + (1, v.shape[-1]))
    bf = lambda w: w.astype(BF16)
    ffn1 = [rows(ffn1_norm), bf(ffn1_w_gate), bf(ffn1_w_up), bf(ffn1_w_down), rows(mix_norm), _proj_weight(w_in)]
    ffn2 = [bf(w_out), rows(ffn2_norm), bf(ffn2_w_gate), bf(ffn2_w_up), bf(ffn2_w_down), rows(ple_norm),
            bf(ple_w_gate), bf(ple_w_proj)]
    mla = [rows(mla_q_norm), _wuq_weight(mla_w_uq), rows(mla_kv_norm), _wukv_weight(mla_w_ukv)]
    na_tables = _na_bias_tables(na_bias)
    gains = rows(group_norm)
    p3 = p.reshape(DEPTH, n_tok, PLE_DIM)

    x2 = x.reshape(n_tok, d_model)
    for i in range(DEPTH):
        at = lambda arrs: [(a, (i,)) for a in arrs]
        gain = lambda g: (gains, (i, g))
        x2, z_na, z_swa, z_dil, z_mla = _ffn1_proj(x2, at(ffn1), cos64, sin64)
        seq_view = lambda z: z.reshape(batch, seq, z.shape[-1])
        y_na = _na_mixer(seq_view(z_na), (na_tables, (i,)), gain(0))
        y_swa = _swa_mixer(i, seq_view(z_swa), swa_sink, gain(1))
        y_dil = _dil_mixer(seq_view(z_dil), gain(2))
        y_mla = _mla_mixer(seq_view(z_mla), at(mla) + [(c_mla, ()), (s_mla, ()), gain(3)])
        ys = [y.reshape(n_tok, GROUP_WIDTH) for y in (y_na, y_swa, y_dil, y_mla)]
        x2 = _out_ffn2_ple(i == DEPTH - 1, i, x2, ys, p3, at(ffn2) + [(final_norm.reshape(1, -1), ())])
    return x2.reshape(batch, seq, d_model)
```

```python
import functools
import math

import numpy as np
import jax
import jax.numpy as jnp
from jax import lax
from jax.experimental import pallas as pl
from jax.experimental.pallas import tpu as pltpu

F32 = jnp.float32
BF16 = jnp.bfloat16

LANES = 128
VMEM_LIMIT_BYTES = 58 * 1024 * 1024

D_MODEL = 1024
SEQ = 2048
DEPTH = 4
HEAD_DIM = 64
GROUP_WIDTH = 256
D_FF = 2816
FF_CHUNK = 256
N_FF_CHUNKS = D_FF // FF_CHUNK
PLE_DIM = 256
ROPE_THETA = 10000.0
EPS = 1e-6
NEG_INF = -1e30
LOG2E = math.log2(math.e)

GRID_W = 64
NA_ROWS = 8
NA_COLS = 16
NA_KEYS = NA_ROWS * GRID_W
SWA_HALF = 128
SWA_BLOCK = 128
SWA_KEYS = SWA_BLOCK + 2 * SWA_HALF
DIL_PATTERNS = ((128, 1), (512, 4), (2048, 16))
DIL_STEPS = 64
DIL_BLOCK = 128
DIL_KEYS = DIL_BLOCK + 2 * DIL_STEPS
MLA_Q_LORA = 256
MLA_KV_LORA = 128
MLA_NOPE = 64
MLA_ROPE = 32
MLA_QBLOCK = 256
MLA_VROWS = HEAD_DIM + 16

NA_OFF, SWA_OFF, DIL_OFF, MLA_OFF = 0, 768, 1280, 2048
P_NA, P_SWA, P_DIL, P_MLA = 0, 768, 1536, 2304
P_COLS = 2944
MLA_COLS = 640

TOKEN_TILE = 512
BAND_UNROLL = 4
NA_UNROLL = 8


def _cparams(n_axes=1):
    return pltpu.CompilerParams(dimension_semantics=("arbitrary",) * n_axes,
                                vmem_limit_bytes=VMEM_LIMIT_BYTES)


def _resident(param):
    arr, lead = param
    rest = arr.shape[len(lead):]
    return pl.BlockSpec((None,) * len(lead) + rest, lambda *_: lead + (0,) * len(rest),
                        pipeline_mode=pl.Buffered(1))


def _rms(x, g):
    return x * lax.rsqrt(jnp.mean(x * x, axis=-1, keepdims=True) + EPS) * g


def _dot(a, b):
    return jnp.dot(a, b, preferred_element_type=F32)


def _dot_t(a, b):
    return lax.dot_general(a, b, (((1,), (1,)), ((), ())), preferred_element_type=F32)


def _swiglu_half_step(x, norm_ref, wg_ref, wu_ref, wd_ref, u_ref, acc_ref):
    u_ref[...] = _rms(x, norm_ref[...]).astype(BF16)
    acc_ref[...] = jnp.zeros_like(acc_ref)

    for j in range(N_FF_CHUNKS):
        cols = slice(j * FF_CHUNK, (j + 1) * FF_CHUNK)
        u = u_ref[...]
        g = _dot(u, wg_ref[:, cols])
        up = _dot(u, wu_ref[:, cols])
        a = (g * jax.nn.sigmoid(g) * up).astype(BF16)
        acc_ref[...] += _dot(a, wd_ref[cols, :])
    return x + 0.5 * acc_ref[...]


def _rope_tile(x, cos, sin_signed):
    lane = lax.broadcasted_iota(jnp.int32, x.shape, 1)
    first_half = (lane % HEAD_DIM) < (HEAD_DIM // 2)
    partner = jnp.where(first_half, pltpu.roll(x, LANES - HEAD_DIM // 2, 1), pltpu.roll(x, HEAD_DIM // 2, 1))
    return x * cos + partner * sin_signed


def _ffn1_proj_kernel(x_ref, n1_ref, wg_ref, wu_ref, wd_ref, mixn_ref, win_ref, cos_ref, sin_ref,
                      xo_ref, zna_ref, zswa_ref, zdil_ref, zmla_ref, u_ref, acc_ref):
    x = _swiglu_half_step(x_ref[...], n1_ref, wg_ref, wu_ref, wd_ref, u_ref, acc_ref)
    xo_ref[...] = x
    u_ref[...] = _rms(x, mixn_ref[...]).astype(BF16)
    cos = cos_ref[...]
    sin = sin_ref[...]
    q_scale = HEAD_DIM ** -0.5 * LOG2E

    def proj(col, width):
        return _dot(u_ref[...], win_ref[:, col:col + width])

    def roped(col, scale):
        z = proj(col, 2 * LANES)
        halves = [_rope_tile(z[:, h * LANES:(h + 1) * LANES], cos, sin) for h in range(2)]
        return jnp.concatenate(halves, axis=1) * scale

    zna_ref[:, 0:256] = (proj(P_NA, 256) * q_scale).astype(BF16)
    zna_ref[:, 256:768] = proj(P_NA + 256, 512).astype(BF16)
    zswa_ref[:, 0:256] = roped(P_SWA, q_scale).astype(BF16)
    zswa_ref[:, 256:512] = roped(P_SWA + 256, 1.0).astype(BF16)
    zswa_ref[:, 512:768] = proj(P_SWA + 512, 256).astype(BF16)
    zdil_ref[:, 0:256] = roped(P_DIL, q_scale)
    zdil_ref[:, 256:512] = roped(P_DIL + 256, 1.0)
    zdil_ref[:, 512:768] = proj(P_DIL + 512, 256)
    zmla_ref[:, 0:384] = proj(P_MLA, 384)
    zmla_ref[:, 384:640] = proj(P_MLA + 384, 256)


def _out_ffn2_ple_kernel(final, x_ref, yna_ref, yswa_ref, ydil_ref, ymla_ref, p_ref, wout_ref,
                         n2_ref, wg_ref, wu_ref, wd_ref, plen_ref, plewg_ref, plewp_ref, fin_ref,
                         xo_ref, u_ref, acc_ref):
    x = x_ref[...]
    mix = _dot(yna_ref[...], wout_ref[0:256, :])
    mix += _dot(yswa_ref[...], wout_ref[256:512, :])
    mix += _dot(ydil_ref[...], wout_ref[512:768, :])
    mix += _dot(ymla_ref[...], wout_ref[768:1024, :])
    x = x + mix
    x = _swiglu_half_step(x, n2_ref, wg_ref, wu_ref, wd_ref, u_ref, acc_ref)
    gate = jax.nn.sigmoid(_dot(_rms(x, plen_ref[...]).astype(BF16), plewg_ref[...]))
    x = x + gate * _dot(p_ref[...].astype(BF16), plewp_ref[...])
    if final:
        x = _rms(x, fin_ref[...])
    xo_ref[...] = x


def _pair_attention(items):
    def stack_heads(q):
        lane = lax.broadcasted_iota(jnp.int32, q.shape, 1)
        zero = jnp.zeros_like(q)
        return jnp.concatenate([jnp.where(lane < HEAD_DIM, q, zero), jnp.where(lane >= HEAD_DIM, q, zero)], axis=0)

    def scores(st, it):
        q, k, _, bias2, _ = it
        st["s"] = _dot_t(stack_heads(q()), k()) + bias2()

    def row_max(st, it):
        m = jnp.max(st["s"], axis=-1, keepdims=True)
        st["m"] = m if it[4] is None else jnp.maximum(m, it[4])

    def probs(st, it):
        p = jnp.exp2(st.pop("s") - st["m"])
        l = jnp.sum(p, axis=-1, keepdims=True)
        st["l"] = l if it[4] is None else l + jnp.exp2(it[4] - st["m"])
        st["p"] = p.astype(BF16)

    def values(st, it):
        st["o"] = _dot(st.pop("p"), it[2]())

    stages = (scores, row_max, probs, values)
    states = [{} for _ in items]
    for wave in range(len(items) + len(stages) - 1):
        for s in reversed(range(len(stages))):
            i = wave - s
            if 0 <= i < len(items):
                stages[s](states[i], items[i])
    return [(st["m"], st["l"], st["o"]) for st in states]


def _merge_heads(x2, rows):
    top, bot = x2[:rows], x2[rows:]
    if top.shape[1] == 1:
        top = jnp.broadcast_to(top, (rows, LANES))
        bot = jnp.broadcast_to(bot, (rows, LANES))
    lane = lax.broadcasted_iota(jnp.int32, (rows, LANES), 1)
    return jnp.where(lane < HEAD_DIM, top, bot)


def _band_bias(offset, n_q, n_k, half_width):
    delta = offset + lax.broadcasted_iota(jnp.int32, (n_q, n_k), 1) - lax.broadcasted_iota(jnp.int32, (n_q, n_k), 0)
    bias = jnp.where(jnp.abs(delta) <= half_width, 0.0, NEG_INF).astype(F32)
    return jnp.concatenate([bias, bias], axis=0)


def _na_kernel(z_ref, bias_ref, gn_ref, o_ref):
    rows = SEQ // GRID_W

    def row_items(r):
        rs = jnp.clip(r - NA_ROWS // 2, 0, rows - NA_ROWS)
        shift = rs - r + NA_ROWS - 1
        q_rows = pl.ds(pl.multiple_of(r * GRID_W, GRID_W), GRID_W)
        k_rows = pl.ds(pl.multiple_of(rs * GRID_W, GRID_W), NA_KEYS)
        items = []
        for pr in range(2):
            tile = lambda rows_, base, pr=pr: z_ref[0, rows_, base + pr * LANES:base + (pr + 1) * LANES]
            items.append((functools.partial(tile, q_rows, 0), functools.partial(tile, k_rows, 256),
                          functools.partial(tile, k_rows, 512), lambda pr=pr: bias_ref[pr, shift], None))
        return q_rows, items

    def body(it, carry):
        loaded = [row_items(it * NA_UNROLL + u) for u in range(NA_UNROLL)]
        stats = _pair_attention([item for _, items in loaded for item in items])
        outs = [_merge_heads(o / l, GRID_W) for _, l, o in stats]
        ys = [_rms(jnp.concatenate(outs[2 * u:2 * u + 2], axis=1), gn_ref[...]).astype(BF16)
              for u in range(NA_UNROLL)]
        for (q_rows, _), y in zip(loaded, ys):
            o_ref[0, q_rows, :] = y
        return carry

    lax.fori_loop(0, rows // NA_UNROLL, body, 0)


def _swa_kernel(layer, sink_ref, z_ref, gn_ref, o_ref):
    head_row = lax.broadcasted_iota(jnp.int32, (2 * SWA_BLOCK, 1), 0) < SWA_BLOCK
    sinks = [jnp.where(head_row, sink_ref[layer, 2 * g], sink_ref[layer, 2 * g + 1]).astype(F32) * LOG2E
             for g in range(2)]

    def block_items(i):
        s0 = pl.multiple_of(i * SWA_BLOCK, SWA_BLOCK)
        ks = pl.multiple_of(jnp.clip(s0 - SWA_HALF, 0, SEQ - SWA_KEYS), SWA_BLOCK)
        q_rows, k_rows = pl.ds(s0, SWA_BLOCK), pl.ds(ks, SWA_KEYS)
        bias2 = functools.cache(functools.partial(_band_bias, ks - s0, SWA_BLOCK, SWA_KEYS, SWA_HALF))
        items = []
        for g in range(2):
            tile = lambda rows_, base, g=g: z_ref[0, rows_, base + g * LANES:base + (g + 1) * LANES]
            items.append((functools.partial(tile, q_rows, 0), functools.partial(tile, k_rows, 256),
                          functools.partial(tile, k_rows, 512), bias2, sinks[g]))
        return q_rows, items

    def body(it, carry):
        loaded = [block_items(it * BAND_UNROLL + u) for u in range(BAND_UNROLL)]
        stats = _pair_attention([item for _, items in loaded for item in items])
        outs = [_merge_heads(o / l, SWA_BLOCK) for _, l, o in stats]
        ys = [_rms(jnp.concatenate(outs[2 * u:2 * u + 2], axis=1), gn_ref[...]).astype(BF16)
              for u in range(BAND_UNROLL)]
        for (q_rows, _), y in zip(loaded, ys):
            o_ref[0, q_rows, :] = y
        return carry

    lax.fori_loop(0, SEQ // SWA_BLOCK // BAND_UNROLL, body, 0)


def _dil_kernel(q0_ref, q1_ref, k0_ref, k1_ref, v0_ref, v1_ref, gn_ref, o_ref, m_ref, l_ref, acc_ref):
    qkv_refs = ((q0_ref, k0_ref, v0_ref), (q1_ref, k1_ref, v1_ref))

    def run_pattern(first, dil, steps):
        res_len = SEQ // dil
        blocks_per_res = res_len // DIL_BLOCK
        n_keys = min(DIL_KEYS, res_len)

        def block_items(idx):
            res = idx // blocks_per_res
            q_lo = (idx % blocks_per_res) * DIL_BLOCK
            k_lo = jnp.clip(q_lo - steps, 0, res_len - n_keys)
            if dil == 1:
                q_rows = pl.ds(pl.multiple_of(q_lo, DIL_BLOCK), DIL_BLOCK)
                k_rows = pl.ds(pl.multiple_of(k_lo, DIL_STEPS), n_keys)
            else:
                q_rows = pl.ds(res + dil * q_lo, DIL_BLOCK, stride=dil)
                k_rows = pl.ds(res + dil * k_lo, n_keys, stride=dil)
            bias2 = functools.cache(functools.partial(_band_bias, k_lo - q_lo, DIL_BLOCK, n_keys, steps))
            items = []
            for pr in range(2):
                tile = lambda ref, rows_: ref[0, rows_, :].astype(BF16)
                q_ref, k_ref, v_ref = qkv_refs[pr]
                items.append((functools.partial(tile, q_ref, q_rows), functools.partial(tile, k_ref, k_rows),
                              functools.partial(tile, v_ref, k_rows), bias2, None))
            return q_rows, items

        def merge(new, pr, q_rows):
            m, l, o = (_merge_heads(t, DIL_BLOCK) for t in new)
            if first:
                return m, l, o
            m_old = m_ref[pr, q_rows, :]
            m_new = jnp.maximum(m_old, m)
            w_old = jnp.exp2(m_old - m_new)
            w_blk = jnp.exp2(m - m_new)
            return m_new, l_ref[pr, q_rows, :] * w_old + l * w_blk, acc_ref[pr, q_rows, :] * w_old + o * w_blk

        unroll = NA_UNROLL if dil == 1 else BAND_UNROLL

        def body(it, carry):
            loaded = [block_items(it * unroll + u) for u in range(unroll)]
            stats = _pair_attention([item for _, items in loaded for item in items])
            merged = [merge(new, i % 2, loaded[i // 2][0]) for i, new in enumerate(stats)]
            for i, (m, l, o) in enumerate(merged):
                pr, q_rows = i % 2, loaded[i // 2][0]
                m_ref[pr, q_rows, :] = m
                l_ref[pr, q_rows, :] = l
                acc_ref[pr, q_rows, :] = o
            return carry

        lax.fori_loop(0, dil * blocks_per_res // unroll, body, 0)

    for p_idx, (window, dil) in enumerate(DIL_PATTERNS):
        run_pattern(p_idx == 0, dil, window // 2 // dil)

    y = jnp.concatenate([acc_ref[0] / l_ref[0], acc_ref[1] / l_ref[1]], axis=1)
    o_ref[0] = _rms(y, gn_ref[...]).astype(BF16)


def _mla_kernel(z_ref, qn_ref, wuq_ref, kvn_ref, wukv_ref, c_ref, s_ref, gn_ref, o_ref,
                q_scr, k_scr, vt_scr, sa_scr, sb_scr, ma_scr, mb_scr):
    scale = (MLA_NOPE + MLA_ROPE) ** -0.5 * LOG2E
    c_tab = c_ref[...]
    s_tab = s_ref[...]
    cqn = _rms(z_ref[0, :, 0:MLA_Q_LORA], qn_ref[...]).astype(BF16)
    for h in range(4):
        g = _dot(cqn, wuq_ref[:, h * 256:(h + 1) * 256])
        q_scr[h] = ((g[:, :LANES] * c_tab + g[:, LANES:] * s_tab) * scale).astype(BF16)
    ckvn = _rms(z_ref[0, :, MLA_Q_LORA:MLA_Q_LORA + MLA_KV_LORA], kvn_ref[...]).astype(BF16)
    k_pe = z_ref[0, :, 384:512] * c_tab + z_ref[0, :, 512:640] * s_tab
    for h in range(4):
        k_scr[h] = (_dot(ckvn, wukv_ref[:, h * LANES:(h + 1) * LANES]) + k_pe).astype(BF16)
    vt = _dot(ckvn, wukv_ref[:, 512:768]).T
    pad_row = lax.broadcasted_iota(jnp.int32, (MLA_VROWS - HEAD_DIM, SEQ), 0)
    ones_then_zeros = jnp.where(pad_row == 0, 1.0, 0.0).astype(F32)
    for h in range(4):
        vt_scr[h] = jnp.concatenate([vt[h * HEAD_DIM:(h + 1) * HEAD_DIM], ones_then_zeros], axis=0).astype(BF16)

    heads = range(4)
    n_blocks = SEQ // MLA_QBLOCK

    def scores(h, blk, s_scr, m_scr):
        q0 = pl.multiple_of(blk * MLA_QBLOCK, MLA_QBLOCK)
        s = _dot_t(k_scr[h], q_scr[h, pl.ds(q0, MLA_QBLOCK), :])
        s_scr[h] = s
        m_scr[h] = jnp.broadcast_to(jnp.max(s, axis=0, keepdims=True), (8, MLA_QBLOCK))

    def finish(cur, blk, nxt, nxt_blk):
        yts = []
        for h in heads:
            scores(h, nxt_blk, *nxt)
            s_scr, m_scr = cur
            p = jnp.exp2(s_scr[h] - m_scr[h][0:1, :])
            ol = _dot(vt_scr[h], p.astype(BF16))
            yts.append(ol[:HEAD_DIM] / ol[HEAD_DIM:HEAD_DIM + 1])
        y = jnp.concatenate(yts, axis=0).T
        q0 = pl.multiple_of(blk * MLA_QBLOCK, MLA_QBLOCK)
        o_ref[0, pl.ds(q0, MLA_QBLOCK), :] = _rms(y, gn_ref[...]).astype(BF16)

    buf_a, buf_b = (sa_scr, ma_scr), (sb_scr, mb_scr)
    for h in heads:
        scores(h, 0, *buf_a)

    def two_blocks(j, carry):
        finish(buf_a, 2 * j, buf_b, 2 * j + 1)
        finish(buf_b, 2 * j + 1, buf_a, (2 * j + 2) % n_blocks)
        return carry

    lax.fori_loop(0, n_blocks // 2, two_blocks, 0)


def _rope_tables_64():
    half = HEAD_DIM // 2
    inv_freq = np.float32(ROPE_THETA) ** (-np.arange(half, dtype=np.float32) / np.float32(half))
    ang = np.arange(SEQ, dtype=np.float32)[:, None] * inv_freq[None, :]
    cos, sin = np.cos(ang).astype(np.float32), np.sin(ang).astype(np.float32)
    cos_h = np.concatenate([cos, cos], axis=1)
    sin_h = np.concatenate([-sin, sin], axis=1)
    return jnp.asarray(np.tile(cos_h, (1, 2))), jnp.asarray(np.tile(sin_h, (1, 2)))


def _rope_tables_mla():
    half = MLA_ROPE // 2
    inv_freq = np.float32(ROPE_THETA) ** (-np.arange(half, dtype=np.float32) / np.float32(half))
    ang = np.arange(SEQ, dtype=np.float32)[:, None] * inv_freq[None, :]
    cos, sin = np.cos(ang).astype(np.float32), np.sin(ang).astype(np.float32)
    ones = np.ones((SEQ, MLA_NOPE), np.float32)
    zeros_n = np.zeros((SEQ, MLA_NOPE), np.float32)
    zeros_p = np.zeros((SEQ, LANES - MLA_NOPE - MLA_ROPE), np.float32)
    c_tab = np.concatenate([ones, cos, cos, zeros_p], axis=1)
    s_tab = np.concatenate([zeros_n, -sin, sin, zeros_p], axis=1)
    return jnp.asarray(c_tab), jnp.asarray(s_tab)


def _cols(w, pieces):
    parts = [jnp.zeros(w.shape[:-1] + (pc,), w.dtype) if isinstance(pc, int) else w[..., pc[0]:pc[0] + pc[1]]
             for pc in pieces]
    return jnp.concatenate(parts, axis=-1)


def _swapped(start, width):
    return [(start + width // 2, width // 2), (start, width // 2)]


def _proj_weight(w_in):
    pad = LANES - MLA_NOPE - MLA_ROPE
    kpe = MLA_OFF + MLA_Q_LORA + MLA_KV_LORA
    k0, k1 = (SWA_OFF + 256, 64), (SWA_OFF + 320, 64)
    v0, v1 = (SWA_OFF + 384, 64), (SWA_OFF + 448, 64)
    pieces = ([(NA_OFF, 768), (SWA_OFF, 256), k0, k0, k1, k1, v0, v0, v1, v1, (DIL_OFF, 768),
               (MLA_OFF, MLA_Q_LORA + MLA_KV_LORA), MLA_NOPE, (kpe, MLA_ROPE), pad, MLA_NOPE]
              + _swapped(kpe, MLA_ROPE) + [pad])
    w = _cols(w_in.astype(BF16), pieces)
    assert w.shape[-1] == P_COLS
    return w


def _wuq_weight(w_uq):
    pad = LANES - MLA_NOPE - MLA_ROPE
    pieces = []
    for h in range(4):
        base = h * (MLA_NOPE + MLA_ROPE)
        pieces += [(base, MLA_NOPE + MLA_ROPE), pad, MLA_NOPE] + _swapped(base + MLA_NOPE, MLA_ROPE) + [pad]
    return _cols(w_uq.astype(BF16), pieces)


def _wukv_weight(w_ukv):
    pieces = []
    for h in range(4):
        pieces += [(h * 128, MLA_NOPE), LANES - MLA_NOPE]
    pieces += [(h * 128 + MLA_NOPE, HEAD_DIM) for h in range(4)]
    return _cols(w_ukv.astype(BF16), pieces)


def _na_bias_tables(bias):
    n_layers = bias.shape[0]
    qc = np.arange(GRID_W)[:, None]
    kc = np.arange(GRID_W)[None, :]
    ws = np.clip(qc - NA_COLS // 2, 0, GRID_W - NA_COLS)
    valid = jnp.asarray((kc >= ws) & (kc < ws + NA_COLS))
    lo = GRID_W - NA_COLS
    span = 2 * GRID_W - 1
    padded = jnp.pad(bias, ((0, 0), (0, 0), (0, 0), (lo, lo + 1)))
    skew = jnp.broadcast_to(padded[..., None, :], padded.shape[:-1] + (GRID_W, span + 1))
    skew = skew.reshape(padded.shape[:-1] + (GRID_W * (span + 1),))[..., :GRID_W * span]
    toep = skew.reshape(padded.shape[:-1] + (GRID_W, span))[..., GRID_W - 1:]
    toep = jnp.where(valid, toep * LOG2E, NEG_INF)
    g = jnp.stack([toep[:, :, s:s + NA_ROWS] for s in range(NA_ROWS)], axis=2)
    g = g.transpose(0, 1, 2, 4, 3, 5).reshape(n_layers, 2, 2, NA_ROWS, GRID_W, NA_KEYS)
    return g.transpose(0, 1, 3, 2, 4, 5).reshape(n_layers, 2, NA_ROWS, 2 * GRID_W, NA_KEYS)


def _tok_spec(width):
    return pl.BlockSpec((TOKEN_TILE, width), lambda i: (i, 0))


def _ffn1_proj(x2, params, cos, sin):
    n_tok = x2.shape[0]
    tiles_per_seq = SEQ // TOKEN_TILE
    pos_spec = pl.BlockSpec((TOKEN_TILE, LANES), lambda i: (i % tiles_per_seq, 0))
    return pl.pallas_call(
        _ffn1_proj_kernel,
        grid=(n_tok // TOKEN_TILE,),
        in_specs=[_tok_spec(D_MODEL)] + [_resident(prm) for prm in params] + [pos_spec, pos_spec],
        out_specs=[_tok_spec(D_MODEL), _tok_spec(768), _tok_spec(768), _tok_spec(768), _tok_spec(MLA_COLS)],
        out_shape=[jax.ShapeDtypeStruct((n_tok, D_MODEL), F32),
                   jax.ShapeDtypeStruct((n_tok, 768), BF16),
                   jax.ShapeDtypeStruct((n_tok, 768), BF16),
                   jax.ShapeDtypeStruct((n_tok, 768), F32),
                   jax.ShapeDtypeStruct((n_tok, MLA_COLS), F32)],
        scratch_shapes=[pltpu.VMEM((TOKEN_TILE, D_MODEL), BF16), pltpu.VMEM((TOKEN_TILE, D_MODEL), F32)],
        compiler_params=_cparams(),
        name="ffn1_proj",
    )(x2, *[arr for arr, _ in params], cos, sin)


def _out_ffn2_ple(final, layer, x2, ys, p3, params):
    n_tok = x2.shape[0]
    p_spec = pl.BlockSpec((None, TOKEN_TILE, PLE_DIM), lambda i: (layer, i, 0))
    return pl.pallas_call(
        functools.partial(_out_ffn2_ple_kernel, final),
        grid=(n_tok // TOKEN_TILE,),
        in_specs=[_tok_spec(D_MODEL)] + [_tok_spec(GROUP_WIDTH)] * 4 + [p_spec]
        + [_resident(prm) for prm in params],
        out_specs=_tok_spec(D_MODEL),
        out_shape=jax.ShapeDtypeStruct((n_tok, D_MODEL), F32),
        scratch_shapes=[pltpu.VMEM((TOKEN_TILE, D_MODEL), BF16), pltpu.VMEM((TOKEN_TILE, D_MODEL), F32)],
        compiler_params=_cparams(),
        name="out_ffn2_ple",
    )(x2, *ys, p3, *[arr for arr, _ in params])


def _seq_spec(width):
    return pl.BlockSpec((1, SEQ, width), lambda b: (b, 0, 0))


def _y_shape(batch):
    return jax.ShapeDtypeStruct((batch, SEQ, GROUP_WIDTH), BF16)


def _na_mixer(z, bias_tab, gn):
    batch = z.shape[0]
    return pl.pallas_call(
        _na_kernel, grid=(batch,),
        in_specs=[_seq_spec(768), _resident(bias_tab), _resident(gn)],
        out_specs=_seq_spec(GROUP_WIDTH), out_shape=_y_shape(batch),
        compiler_params=_cparams(), name="na_mixer",
    )(z, bias_tab[0], gn[0])


def _swa_mixer(layer, z, sinks, gn):
    batch = z.shape[0]
    return pl.pallas_call(
        functools.partial(_swa_kernel, layer), grid=(batch,),
        in_specs=[pl.BlockSpec(memory_space=pltpu.SMEM), _seq_spec(768), _resident(gn)],
        out_specs=_seq_spec(GROUP_WIDTH), out_shape=_y_shape(batch),
        compiler_params=_cparams(), name="swa_mixer",
    )(sinks, z, gn[0])


def _dil_mixer(z, gn):
    batch = z.shape[0]
    pair_specs = [pl.BlockSpec((1, SEQ, LANES), functools.partial(lambda c, b: (b, 0, c), c)) for c in range(6)]
    return pl.pallas_call(
        _dil_kernel, grid=(batch,),
        in_specs=pair_specs + [_resident(gn)],
        out_specs=_seq_spec(GROUP_WIDTH), out_shape=_y_shape(batch),
        scratch_shapes=[pltpu.VMEM((2, SEQ, LANES), F32)] * 3,
        compiler_params=_cparams(), name="dil_mixer",
    )(z, z, z, z, z, z, gn[0])


def _mla_mixer(z, params):
    batch = z.shape[0]
    return pl.pallas_call(
        _mla_kernel, grid=(batch,),
        in_specs=[_seq_spec(MLA_COLS)] + [_resident(prm) for prm in params],
        out_specs=_seq_spec(GROUP_WIDTH), out_shape=_y_shape(batch),
        scratch_shapes=[pltpu.VMEM((4, SEQ, LANES), BF16), pltpu.VMEM((4, SEQ, LANES), BF16),
                        pltpu.VMEM((4, MLA_VROWS, SEQ), BF16),
                        pltpu.VMEM((4, SEQ, MLA_QBLOCK), F32), pltpu.VMEM((4, SEQ, MLA_QBLOCK), F32),
                        pltpu.VMEM((4, 8, MLA_QBLOCK), F32), pltpu.VMEM((4, 8, MLA_QBLOCK), F32)],
        compiler_params=_cparams(), name="mla_mixer",
    )(z, *[arr for arr, _ in params])


def kernel(x, p, ffn1_norm, ffn1_w_gate, ffn1_w_up, ffn1_w_down, mix_norm, w_in, na_bias, swa_sink, mla_q_norm, mla_w_uq, mla_kv_norm, mla_w_ukv, group_norm, w_out, ffn2_norm, ffn2_w_gate, ffn2_w_up, ffn2_w_down, ple_norm, ple_w_gate, ple_w_proj, final_norm):
    batch, seq, d_model = x.shape
    assert (seq, d_model) == (SEQ, D_MODEL) and x.dtype == F32
    n_tok = batch * seq
    cos64, sin64 = _rope_tables_64()
    c_mla, s_mla = _rope_tables_mla()
    rows = lambda v: v.reshape(v.shape[:-1] + (1, v.shape[-1]))
    bf = lambda w: w.astype(BF16)
    ffn1 = [rows(ffn1_norm), bf(ffn1_w_gate), bf(ffn1_w_up), bf(ffn1_w_down), rows(mix_norm), _proj_weight(w_in)]
    ffn2 = [bf(w_out), rows(ffn2_norm), bf(ffn2_w_gate), bf(ffn2_w_up), bf(ffn2_w_down), rows(ple_norm),
            bf(ple_w_gate), bf(ple_w_proj)]
    mla = [rows(mla_q_norm), _wuq_weight(mla_w_uq), rows(mla_kv_norm), _wukv_weight(mla_w_ukv)]
    na_tables = _na_bias_tables(na_bias)
    gains = rows(group_norm)
    p3 = p.reshape(DEPTH, n_tok, PLE_DIM)

    x2 = x.reshape(n_tok, d_model)
    for i in range(DEPTH):
        at = lambda arrs: [(a, (i,)) for a in arrs]
        gain = lambda g: (gains, (i, g))
        x2, z_na, z_swa, z_dil, z_mla = _ffn1_proj(x2, at(ffn1), cos64, sin64)
        seq_view = lambda z: z.reshape(batch, seq, z.shape[-1])
        y_na = _na_mixer(seq_view(z_na), (na_tables, (i,)), gain(0))
        y_swa = _swa_mixer(i, seq_view(z_swa), swa_sink, gain(1))
        y_dil = _dil_mixer(seq_view(z_dil), gain(2))
        y_mla = _mla_mixer(seq_view(z_mla), at(mla) + [(c_mla, ()), (s_mla, ()), gain(3)])
        ys = [y.reshape(n_tok, GROUP_WIDTH) for y in (y_na, y_swa, y_dil, y_mla)]
        x2 = _out_ffn2_ple(i == DEPTH - 1, i, x2, ys, p3, at(ffn2) + [(final_norm.reshape(1, -1), ())])
    return x2.reshape(batch, seq, d_model)
```

```python
import functools
import math

import numpy as np
import jax
import jax.numpy as jnp
from jax import lax
from jax.experimental import pallas as pl
from jax.experimental.pallas import tpu as pltpu

F32 = jnp.float32
BF16 = jnp.bfloat16

LANES = 128
VMEM_LIMIT_BYTES = 58 * 1024 * 1024

D_MODEL = 1024
SEQ = 2048
DEPTH = 4
HEAD_DIM = 64
GROUP_WIDTH = 256
D_FF = 2816
FF_CHUNK = 256
N_FF_CHUNKS = D_FF // FF_CHUNK
PLE_DIM = 256
ROPE_THETA = 10000.0
EPS = 1e-6
NEG_INF = -1e30
LOG2E = math.log2(math.e)

GRID_W = 64
NA_ROWS = 8
NA_COLS = 16
NA_KEYS = NA_ROWS * GRID_W
SWA_HALF = 128
SWA_BLOCK = 128
SWA_KEYS = SWA_BLOCK + 2 * SWA_HALF
DIL_PATTERNS = ((128, 1), (512, 4), (2048, 16))
DIL_STEPS = 64
DIL_BLOCK = 128
DIL_KEYS = DIL_BLOCK + 2 * DIL_STEPS
DIL_CLASSES = 4
MLA_Q_LORA = 256
MLA_KV_LORA = 128
MLA_NOPE = 64
MLA_ROPE = 32
MLA_QBLOCK = 256
MLA_VROWS = HEAD_DIM + 16

NA_OFF, SWA_OFF, DIL_OFF, MLA_OFF = 0, 768, 1280, 2048
P_NA, P_SWA, P_DIL, P_MLA = 0, 768, 1536, 2304
P_COLS = 2944
MLA_COLS = 640

TOKEN_TILE = 512
BAND_UNROLL = 4
NA_UNROLL = 8


def _cparams(n_axes=1):
    return pltpu.CompilerParams(dimension_semantics=("arbitrary",) * n_axes,
                                vmem_limit_bytes=VMEM_LIMIT_BYTES)


def _resident(param):
    arr, lead = param
    rest = arr.shape[len(lead):]
    return pl.BlockSpec((None,) * len(lead) + rest, lambda *_: lead + (0,) * len(rest),
                        pipeline_mode=pl.Buffered(1))


def _rms(x, g):
    return x * lax.rsqrt(jnp.mean(x * x, axis=-1, keepdims=True) + EPS) * g


def _dot(a, b):
    return jnp.dot(a, b, preferred_element_type=F32)


def _dot_t(a, b):
    return lax.dot_general(a, b, (((1,), (1,)), ((), ())), preferred_element_type=F32)


def _swiglu_half_step(x, norm_ref, wg_ref, wu_ref, wd_ref, u_ref, acc_ref):
    u_ref[...] = _rms(x, norm_ref[...]).astype(BF16)
    acc_ref[...] = jnp.zeros_like(acc_ref)

    for j in range(N_FF_CHUNKS):
        cols = slice(j * FF_CHUNK, (j + 1) * FF_CHUNK)
        u = u_ref[...]
        g = _dot(u, wg_ref[:, cols])
        up = _dot(u, wu_ref[:, cols])
        a = (g * jax.nn.sigmoid(g) * up).astype(BF16)
        acc_ref[...] += _dot(a, wd_ref[cols, :])
    return x + 0.5 * acc_ref[...]


def _rope_tile(x, cos, sin_signed):
    lane = lax.broadcasted_iota(jnp.int32, x.shape, 1)
    first_half = (lane % HEAD_DIM) < (HEAD_DIM // 2)
    partner = jnp.where(first_half, pltpu.roll(x, LANES - HEAD_DIM // 2, 1), pltpu.roll(x, HEAD_DIM // 2, 1))
    return x * cos + partner * sin_signed


def _ffn1_proj_kernel(x_ref, n1_ref, wg_ref, wu_ref, wd_ref, mixn_ref, win_ref, cos_ref, sin_ref,
                      xo_ref, zna_ref, zswa_ref, zdil_ref, zmla_ref, u_ref, acc_ref):
    x = _swiglu_half_step(x_ref[...], n1_ref, wg_ref, wu_ref, wd_ref, u_ref, acc_ref)
    xo_ref[...] = x
    u_ref[...] = _rms(x, mixn_ref[...]).astype(BF16)
    cos = cos_ref[...]
    sin = sin_ref[...]
    q_scale = HEAD_DIM ** -0.5 * LOG2E

    def proj(col, width):
        return _dot(u_ref[...], win_ref[:, col:col + width])

    def roped(col, scale):
        z = proj(col, 2 * LANES)
        halves = [_rope_tile(z[:, h * LANES:(h + 1) * LANES], cos, sin) for h in range(2)]
        return jnp.concatenate(halves, axis=1) * scale

    zna_ref[:, 0:256] = (proj(P_NA, 256) * q_scale).astype(BF16)
    zna_ref[:, 256:768] = proj(P_NA + 256, 512).astype(BF16)
    zswa_ref[:, 0:256] = roped(P_SWA, q_scale).astype(BF16)
    zswa_ref[:, 256:512] = roped(P_SWA + 256, 1.0).astype(BF16)
    zswa_ref[:, 512:768] = proj(P_SWA + 512, 256).astype(BF16)
    zdil_ref[:, 0:256] = roped(P_DIL, q_scale)
    zdil_ref[:, 256:512] = roped(P_DIL + 256, 1.0)
    zdil_ref[:, 512:768] = proj(P_DIL + 512, 256)
    zmla_ref[:, 0:384] = proj(P_MLA, 384)
    zmla_ref[:, 384:640] = proj(P_MLA + 384, 256)


def _out_ffn2_ple_kernel(final, x_ref, yna_ref, yswa_ref, ydil_ref, ymla_ref, p_ref, wout_ref,
                         n2_ref, wg_ref, wu_ref, wd_ref, plen_ref, plewg_ref, plewp_ref, fin_ref,
                         xo_ref, u_ref, acc_ref):
    x = x_ref[...]
    mix = _dot(yna_ref[...], wout_ref[0:256, :])
    mix += _dot(yswa_ref[...], wout_ref[256:512, :])
    mix += _dot(ydil_ref[...], wout_ref[512:768, :])
    mix += _dot(ymla_ref[...], wout_ref[768:1024, :])
    x = x + mix
    x = _swiglu_half_step(x, n2_ref, wg_ref, wu_ref, wd_ref, u_ref, acc_ref)
    gate = jax.nn.sigmoid(_dot(_rms(x, plen_ref[...]).astype(BF16), plewg_ref[...]))
    x = x + gate * _dot(p_ref[...].astype(BF16), plewp_ref[...])
    if final:
        x = _rms(x, fin_ref[...])
    xo_ref[...] = x


def _pair_attention(items):
    def stack_heads(q):
        lane = lax.broadcasted_iota(jnp.int32, q.shape, 1)
        zero = jnp.zeros_like(q)
        return jnp.concatenate([jnp.where(lane < HEAD_DIM, q, zero), jnp.where(lane >= HEAD_DIM, q, zero)], axis=0)

    def scores(st, it):
        q, k, _, bias2, _ = it
        st["s"] = _dot_t(stack_heads(q()), k()) + bias2()

    def row_max(st, it):
        m = jnp.max(st["s"], axis=-1, keepdims=True)
        st["m"] = m if it[4] is None else jnp.maximum(m, it[4])

    def probs(st, it):
        p = jnp.exp2(st.pop("s") - st["m"])
        l = jnp.sum(p, axis=-1, keepdims=True)
        st["l"] = l if it[4] is None else l + jnp.exp2(it[4] - st["m"])
        st["p"] = p.astype(BF16)

    def values(st, it):
        st["o"] = _dot(st.pop("p"), it[2]())

    stages = (scores, row_max, probs, values)
    states = [{} for _ in items]
    for wave in range(len(items) + len(stages) - 1):
        for s in reversed(range(len(stages))):
            i = wave - s
            if 0 <= i < len(items):
                stages[s](states[i], items[i])
    return [(st["m"], st["l"], st["o"]) for st in states]


def _merge_heads(x2, rows):
    top, bot = x2[:rows], x2[rows:]
    if top.shape[1] == 1:
        top = jnp.broadcast_to(top, (rows, LANES))
        bot = jnp.broadcast_to(bot, (rows, LANES))
    lane = lax.broadcasted_iota(jnp.int32, (rows, LANES), 1)
    return jnp.where(lane < HEAD_DIM, top, bot)


def _band_bias(offset, n_q, n_k, half_width):
    delta = offset + lax.broadcasted_iota(jnp.int32, (n_q, n_k), 1) - lax.broadcasted_iota(jnp.int32, (n_q, n_k), 0)
    bias = jnp.where(jnp.abs(delta) <= half_width, 0.0, NEG_INF).astype(F32)
    return jnp.concatenate([bias, bias], axis=0)


def _na_kernel(z_ref, bias_ref, gn_ref, o_ref):
    rows = SEQ // GRID_W

    def row_items(r):
        rs = jnp.clip(r - NA_ROWS // 2, 0, rows - NA_ROWS)
        shift = rs - r + NA_ROWS - 1
        q_rows = pl.ds(pl.multiple_of(r * GRID_W, GRID_W), GRID_W)
        k_rows = pl.ds(pl.multiple_of(rs * GRID_W, GRID_W), NA_KEYS)
        items = []
        for pr in range(2):
            tile = lambda rows_, base, pr=pr: z_ref[0, rows_, base + pr * LANES:base + (pr + 1) * LANES]
            items.append((functools.partial(tile, q_rows, 0), functools.partial(tile, k_rows, 256),
                          functools.partial(tile, k_rows, 512), lambda pr=pr: bias_ref[pr, shift], None))
        return q_rows, items

    def body(it, carry):
        loaded = [row_items(it * NA_UNROLL + u) for u in range(NA_UNROLL)]
        stats = _pair_attention([item for _, items in loaded for item in items])
        outs = [_merge_heads(o / l, GRID_W) for _, l, o in stats]
        ys = [_rms(jnp.concatenate(outs[2 * u:2 * u + 2], axis=1), gn_ref[...]).astype(BF16)
              for u in range(NA_UNROLL)]
        for (q_rows, _), y in zip(loaded, ys):
            o_ref[0, q_rows, :] = y
        return carry

    lax.fori_loop(0, rows // NA_UNROLL, body, 0)


def _swa_kernel(layer, sink_ref, z_ref, gn_ref, o_ref):
    head_row = lax.broadcasted_iota(jnp.int32, (2 * SWA_BLOCK, 1), 0) < SWA_BLOCK
    sinks = [jnp.where(head_row, sink_ref[layer, 2 * g], sink_ref[layer, 2 * g + 1]).astype(F32) * LOG2E
             for g in range(2)]

    def block_items(i):
        s0 = pl.multiple_of(i * SWA_BLOCK, SWA_BLOCK)
        ks = pl.multiple_of(jnp.clip(s0 - SWA_HALF, 0, SEQ - SWA_KEYS), SWA_BLOCK)
        q_rows, k_rows = pl.ds(s0, SWA_BLOCK), pl.ds(ks, SWA_KEYS)
        bias2 = functools.cache(functools.partial(_band_bias, ks - s0, SWA_BLOCK, SWA_KEYS, SWA_HALF))
        items = []
        for g in range(2):
            tile = lambda rows_, base, g=g: z_ref[0, rows_, base + g * LANES:base + (g + 1) * LANES]
            items.append((functools.partial(tile, q_rows, 0), functools.partial(tile, k_rows, 256),
                          functools.partial(tile, k_rows, 512), bias2, sinks[g]))
        return q_rows, items

    def body(it, carry):
        loaded = [block_items(it * BAND_UNROLL + u) for u in range(BAND_UNROLL)]
        stats = _pair_attention([item for _, items in loaded for item in items])
        outs = [_merge_heads(o / l, SWA_BLOCK) for _, l, o in stats]
        ys = [_rms(jnp.concatenate(outs[2 * u:2 * u + 2], axis=1), gn_ref[...]).astype(BF16)
              for u in range(BAND_UNROLL)]
        for (q_rows, _), y in zip(loaded, ys):
            o_ref[0, q_rows, :] = y
        return carry

    lax.fori_loop(0, SEQ // SWA_BLOCK // BAND_UNROLL, body, 0)


def _dil_kernel(q0_ref, q1_ref, k0_ref, k1_ref, v0_ref, v1_ref, gn_ref, o_ref,
                cls_scr, s1_scr, s2_scr, s3_scr, y_scr):
    in_refs = (q0_ref, q1_ref, k0_ref, k1_ref, v0_ref, v1_ref)
    cls_len = SEQ // DIL_CLASSES

    for arr, ref in enumerate(in_refs):
        for a in range(DIL_CLASSES):
            cls_scr[arr, a * cls_len:(a + 1) * cls_len, :] = ref[0, pl.ds(a, cls_len, stride=DIL_CLASSES), :]

    def run(n_units, unroll, unit_items, out_scr):
        def body(it, carry):
            units = [unit_items(it * unroll + j) for j in range(unroll)]
            stats = _pair_attention([item for _, items in units for item in items])
            for i, new in enumerate(stats):
                rows = units[i // 2][0]
                for stat, val in enumerate(new):
                    out_scr[stat, i % 2, rows, :] = _merge_heads(val, DIL_BLOCK)
            return carry
        lax.fori_loop(0, n_units // unroll, body, 0)

    def band(k_lo, q_lo, n_keys):
        return functools.cache(functools.partial(_band_bias, k_lo - q_lo, DIL_BLOCK, n_keys, DIL_STEPS))

    def p1_unit(u):
        q_lo = pl.multiple_of(u * DIL_BLOCK, DIL_BLOCK)
        k_lo = pl.multiple_of(jnp.clip(q_lo - DIL_STEPS, 0, SEQ - DIL_KEYS), DIL_STEPS)
        q_rows, k_rows = pl.ds(q_lo, DIL_BLOCK), pl.ds(k_lo, DIL_KEYS)
        bias2 = band(k_lo, q_lo, DIL_KEYS)
        tile = lambda ref, rows: ref[0, rows, :].astype(BF16)
        return q_rows, [(functools.partial(tile, in_refs[pr], q_rows), functools.partial(tile, in_refs[2 + pr], k_rows),
                         functools.partial(tile, in_refs[4 + pr], k_rows), bias2, None) for pr in range(2)]

    run(SEQ // DIL_BLOCK, NA_UNROLL, p1_unit, s1_scr)

    cls_tile = lambda arr, rows: cls_scr[arr, rows, :].astype(BF16)

    blocks_per_cls = cls_len // DIL_BLOCK

    def p2_unit(u):
        base = (u // blocks_per_cls) * cls_len
        q_lo = (u % blocks_per_cls) * DIL_BLOCK
        k_lo = jnp.clip(q_lo - DIL_STEPS, 0, cls_len - DIL_KEYS)
        q_rows = pl.ds(pl.multiple_of(base + q_lo, DIL_BLOCK), DIL_BLOCK)
        k_rows = pl.ds(pl.multiple_of(base + k_lo, DIL_STEPS), DIL_KEYS)
        bias2 = band(k_lo, q_lo, DIL_KEYS)
        return q_rows, [(functools.partial(cls_tile, pr, q_rows), functools.partial(cls_tile, 2 + pr, k_rows),
                         functools.partial(cls_tile, 4 + pr, k_rows), bias2, None) for pr in range(2)]

    run(DIL_CLASSES * blocks_per_cls, BAND_UNROLL, p2_unit, s2_scr)

    sub = DIL_PATTERNS[2][1] // DIL_CLASSES
    bias3 = functools.cache(functools.partial(_band_bias, 0, DIL_BLOCK, DIL_BLOCK, DIL_STEPS))

    def p3_unit(u):
        rows = pl.ds((u // sub) * cls_len + (u % sub), DIL_BLOCK, stride=sub)
        return rows, [(functools.partial(cls_tile, pr, rows), functools.partial(cls_tile, 2 + pr, rows),
                       functools.partial(cls_tile, 4 + pr, rows), bias3, None) for pr in range(2)]

    run(DIL_PATTERNS[2][1], BAND_UNROLL, p3_unit, s3_scr)

    def merge_chunk(u, carry):
        a = u // blocks_per_cls
        j0 = (u % blocks_per_cls) * DIL_BLOCK
        cls_rows = pl.ds(pl.multiple_of(a * cls_len + j0, DIL_BLOCK), DIL_BLOCK)
        seq_rows = pl.ds(a + DIL_CLASSES * j0, DIL_BLOCK, stride=DIL_CLASSES)
        ys = []
        for pr in range(2):
            parts = [(s1_scr, seq_rows), (s2_scr, cls_rows), (s3_scr, cls_rows)]
            ms = [scr[0, pr, rows, :] for scr, rows in parts]
            m_all = jnp.maximum(jnp.maximum(ms[0], ms[1]), ms[2])
            ws = [jnp.exp2(m - m_all) for m in ms]
            den = sum(scr[1, pr, rows, :] * w for (scr, rows), w in zip(parts, ws))
            num = sum(scr[2, pr, rows, :] * w for (scr, rows), w in zip(parts, ws))
            ys.append(num / den)
        y = _rms(jnp.concatenate(ys, axis=1), gn_ref[...])
        for pr in range(2):
            y_scr[pr, seq_rows, :] = y[:, pr * LANES:(pr + 1) * LANES]
        return carry

    lax.fori_loop(0, DIL_CLASSES * blocks_per_cls, merge_chunk, 0)
    o_ref[0] = jnp.concatenate([y_scr[0], y_scr[1]], axis=1).astype(BF16)


def _mla_kernel(z_ref, qn_ref, wuq_ref, kvn_ref, wukv_ref, c_ref, s_ref, gn_ref, o_ref,
                q_scr, k_scr, vt_scr, sa_scr, sb_scr, ma_scr, mb_scr):
    scale = (MLA_NOPE + MLA_ROPE) ** -0.5 * LOG2E
    c_tab = c_ref[...]
    s_tab = s_ref[...]
    cqn = _rms(z_ref[0, :, 0:MLA_Q_LORA], qn_ref[...]).astype(BF16)
    for h in range(4):
        g = _dot(cqn, wuq_ref[:, h * 256:(h + 1) * 256])
        q_scr[h] = ((g[:, :LANES] * c_tab + g[:, LANES:] * s_tab) * scale).astype(BF16)
    ckvn = _rms(z_ref[0, :, MLA_Q_LORA:MLA_Q_LORA + MLA_KV_LORA], kvn_ref[...]).astype(BF16)
    k_pe = z_ref[0, :, 384:512] * c_tab + z_ref[0, :, 512:640] * s_tab
    for h in range(4):
        k_scr[h] = (_dot(ckvn, wukv_ref[:, h * LANES:(h + 1) * LANES]) + k_pe).astype(BF16)
    vt = _dot(ckvn, wukv_ref[:, 512:768]).T
    pad_row = lax.broadcasted_iota(jnp.int32, (MLA_VROWS - HEAD_DIM, SEQ), 0)
    ones_then_zeros = jnp.where(pad_row == 0, 1.0, 0.0).astype(F32)
    for h in range(4):
        vt_scr[h] = jnp.concatenate([vt[h * HEAD_DIM:(h + 1) * HEAD_DIM], ones_then_zeros], axis=0).astype(BF16)

    heads = range(4)
    n_blocks = SEQ // MLA_QBLOCK

    def scores(h, blk, s_scr, m_scr):
        q0 = pl.multiple_of(blk * MLA_QBLOCK, MLA_QBLOCK)
        s = _dot_t(k_scr[h], q_scr[h, pl.ds(q0, MLA_QBLOCK), :])
        s_scr[h] = s
        m_scr[h] = jnp.broadcast_to(jnp.max(s, axis=0, keepdims=True), (8, MLA_QBLOCK))

    def finish(cur, blk, nxt, nxt_blk):
        yts = []
        for h in heads:
            scores(h, nxt_blk, *nxt)
            s_scr, m_scr = cur
            p = jnp.exp2(s_scr[h] - m_scr[h][0:1, :])
            ol = _dot(vt_scr[h], p.astype(BF16))
            yts.append(ol[:HEAD_DIM] / ol[HEAD_DIM:HEAD_DIM + 1])
        y = jnp.concatenate(yts, axis=0).T
        q0 = pl.multiple_of(blk * MLA_QBLOCK, MLA_QBLOCK)
        o_ref[0, pl.ds(q0, MLA_QBLOCK), :] = _rms(y, gn_ref[...]).astype(BF16)

    buf_a, buf_b = (sa_scr, ma_scr), (sb_scr, mb_scr)
    for h in heads:
        scores(h, 0, *buf_a)

    def two_blocks(j, carry):
        finish(buf_a, 2 * j, buf_b, 2 * j + 1)
        finish(buf_b, 2 * j + 1, buf_a, (2 * j + 2) % n_blocks)
        return carry

    lax.fori_loop(0, n_blocks // 2, two_blocks, 0)


def _rope_tables_64():
    half = HEAD_DIM // 2
    inv_freq = np.float32(ROPE_THETA) ** (-np.arange(half, dtype=np.float32) / np.float32(half))
    ang = np.arange(SEQ, dtype=np.float32)[:, None] * inv_freq[None, :]
    cos, sin = np.cos(ang).astype(np.float32), np.sin(ang).astype(np.float32)
    cos_h = np.concatenate([cos, cos], axis=1)
    sin_h = np.concatenate([-sin, sin], axis=1)
    return jnp.asarray(np.tile(cos_h, (1, 2))), jnp.asarray(np.tile(sin_h, (1, 2)))


def _rope_tables_mla():
    half = MLA_ROPE // 2
    inv_freq = np.float32(ROPE_THETA) ** (-np.arange(half, dtype=np.float32) / np.float32(half))
    ang = np.arange(SEQ, dtype=np.float32)[:, None] * inv_freq[None, :]
    cos, sin = np.cos(ang).astype(np.float32), np.sin(ang).astype(np.float32)
    ones = np.ones((SEQ, MLA_NOPE), np.float32)
    zeros_n = np.zeros((SEQ, MLA_NOPE), np.float32)
    zeros_p = np.zeros((SEQ, LANES - MLA_NOPE - MLA_ROPE), np.float32)
    c_tab = np.concatenate([ones, cos, cos, zeros_p], axis=1)
    s_tab = np.concatenate([zeros_n, -sin, sin, zeros_p], axis=1)
    return jnp.asarray(c_tab), jnp.asarray(s_tab)


def _cols(w, pieces):
    parts = [jnp.zeros(w.shape[:-1] + (pc,), w.dtype) if isinstance(pc, int) else w[..., pc[0]:pc[0] + pc[1]]
             for pc in pieces]
    return jnp.concatenate(parts, axis=-1)


def _swapped(start, width):
    return [(start + width // 2, width // 2), (start, width // 2)]


def _proj_weight(w_in):
    pad = LANES - MLA_NOPE - MLA_ROPE
    kpe = MLA_OFF + MLA_Q_LORA + MLA_KV_LORA
    k0, k1 = (SWA_OFF + 256, 64), (SWA_OFF + 320, 64)
    v0, v1 = (SWA_OFF + 384, 64), (SWA_OFF + 448, 64)
    pieces = ([(NA_OFF, 768), (SWA_OFF, 256), k0, k0, k1, k1, v0, v0, v1, v1, (DIL_OFF, 768),
               (MLA_OFF, MLA_Q_LORA + MLA_KV_LORA), MLA_NOPE, (kpe, MLA_ROPE), pad, MLA_NOPE]
              + _swapped(kpe, MLA_ROPE) + [pad])
    w = _cols(w_in.astype(BF16), pieces)
    assert w.shape[-1] == P_COLS
    return w


def _wuq_weight(w_uq):
    pad = LANES - MLA_NOPE - MLA_ROPE
    pieces = []
    for h in range(4):
        base = h * (MLA_NOPE + MLA_ROPE)
        pieces += [(base, MLA_NOPE + MLA_ROPE), pad, MLA_NOPE] + _swapped(base + MLA_NOPE, MLA_ROPE) + [pad]
    return _cols(w_uq.astype(BF16), pieces)


def _wukv_weight(w_ukv):
    pieces = []
    for h in range(4):
        pieces += [(h * 128, MLA_NOPE), LANES - MLA_NOPE]
    pieces += [(h * 128 + MLA_NOPE, HEAD_DIM) for h in range(4)]
    return _cols(w_ukv.astype(BF16), pieces)


def _na_bias_tables(bias):
    n_layers = bias.shape[0]
    qc = np.arange(GRID_W)[:, None]
    kc = np.arange(GRID_W)[None, :]
    ws = np.clip(qc - NA_COLS // 2, 0, GRID_W - NA_COLS)
    valid = jnp.asarray((kc >= ws) & (kc < ws + NA_COLS))
    lo = GRID_W - NA_COLS
    span = 2 * GRID_W - 1
    padded = jnp.pad(bias, ((0, 0), (0, 0), (0, 0), (lo, lo + 1)))
    skew = jnp.broadcast_to(padded[..., None, :], padded.shape[:-1] + (GRID_W, span + 1))
    skew = skew.reshape(padded.shape[:-1] + (GRID_W * (span + 1),))[..., :GRID_W * span]
    toep = skew.reshape(padded.shape[:-1] + (GRID_W, span))[..., GRID_W - 1:]
    toep = jnp.where(valid, toep * LOG2E, NEG_INF)
    g = jnp.stack([toep[:, :, s:s + NA_ROWS] for s in range(NA_ROWS)], axis=2)
    g = g.transpose(0, 1, 2, 4, 3, 5).reshape(n_layers, 2, 2, NA_ROWS, GRID_W, NA_KEYS)
    return g.transpose(0, 1, 3, 2, 4, 5).reshape(n_layers, 2, NA_ROWS, 2 * GRID_W, NA_KEYS)


def _tok_spec(width):
    return pl.BlockSpec((TOKEN_TILE, width), lambda i: (i, 0))


def _ffn1_proj(x2, params, cos, sin):
    n_tok = x2.shape[0]
    tiles_per_seq = SEQ // TOKEN_TILE
    pos_spec = pl.BlockSpec((TOKEN_TILE, LANES), lambda i: (i % tiles_per_seq, 0))
    return pl.pallas_call(
        _ffn1_proj_kernel,
        grid=(n_tok // TOKEN_TILE,),
        in_specs=[_tok_spec(D_MODEL)] + [_resident(prm) for prm in params] + [pos_spec, pos_spec],
        out_specs=[_tok_spec(D_MODEL), _tok_spec(768), _tok_spec(768), _tok_spec(768), _tok_spec(MLA_COLS)],
        out_shape=[jax.ShapeDtypeStruct((n_tok, D_MODEL), F32),
                   jax.ShapeDtypeStruct((n_tok, 768), BF16),
                   jax.ShapeDtypeStruct((n_tok, 768), BF16),
                   jax.ShapeDtypeStruct((n_tok, 768), F32),
                   jax.ShapeDtypeStruct((n_tok, MLA_COLS), F32)],
        scratch_shapes=[pltpu.VMEM((TOKEN_TILE, D_MODEL), BF16), pltpu.VMEM((TOKEN_TILE, D_MODEL), F32)],
        compiler_params=_cparams(),
        name="ffn1_proj",
    )(x2, *[arr for arr, _ in params], cos, sin)


def _out_ffn2_ple(final, layer, x2, ys, p3, params):
    n_tok = x2.shape[0]
    p_spec = pl.BlockSpec((None, TOKEN_TILE, PLE_DIM), lambda i: (layer, i, 0))
    return pl.pallas_call(
        functools.partial(_out_ffn2_ple_kernel, final),
        grid=(n_tok // TOKEN_TILE,),
        in_specs=[_tok_spec(D_MODEL)] + [_tok_spec(GROUP_WIDTH)] * 4 + [p_spec]
        + [_resident(prm) for prm in params],
        out_specs=_tok_spec(D_MODEL),
        out_shape=jax.ShapeDtypeStruct((n_tok, D_MODEL), F32),
        scratch_shapes=[pltpu.VMEM((TOKEN_TILE, D_MODEL), BF16), pltpu.VMEM((TOKEN_TILE, D_MODEL), F32)],
        compiler_params=_cparams(),
        name="out_ffn2_ple",
    )(x2, *ys, p3, *[arr for arr, _ in params])


def _seq_spec(width):
    return pl.BlockSpec((1, SEQ, width), lambda b: (b, 0, 0))


def _y_shape(batch):
    return jax.ShapeDtypeStruct((batch, SEQ, GROUP_WIDTH), BF16)


def _na_mixer(z, bias_tab, gn):
    batch = z.shape[0]
    return pl.pallas_call(
        _na_kernel, grid=(batch,),
        in_specs=[_seq_spec(768), _resident(bias_tab), _resident(gn)],
        out_specs=_seq_spec(GROUP_WIDTH), out_shape=_y_shape(batch),
        compiler_params=_cparams(), name="na_mixer",
    )(z, bias_tab[0], gn[0])


def _swa_mixer(layer, z, sinks, gn):
    batch = z.shape[0]
    return pl.pallas_call(
        functools.partial(_swa_kernel, layer), grid=(batch,),
        in_specs=[pl.BlockSpec(memory_space=pltpu.SMEM), _seq_spec(768), _resident(gn)],
        out_specs=_seq_spec(GROUP_WIDTH), out_shape=_y_shape(batch),
        compiler_params=_cparams(), name="swa_mixer",
    )(sinks, z, gn[0])


def _dil_mixer(z, gn):
    batch = z.shape[0]
    pair_specs = [pl.BlockSpec((1, SEQ, LANES), functools.partial(lambda c, b: (b, 0, c), c)) for c in range(6)]
    return pl.pallas_call(
        _dil_kernel, grid=(batch,),
        in_specs=pair_specs + [_resident(gn)],
        out_specs=_seq_spec(GROUP_WIDTH), out_shape=_y_shape(batch),
        scratch_shapes=[pltpu.VMEM((6, SEQ, LANES), F32)] + [pltpu.VMEM((3, 2, SEQ, LANES), F32)] * 3
        + [pltpu.VMEM((2, SEQ, LANES), F32)],
        compiler_params=_cparams(), name="dil_mixer",
    )(z, z, z, z, z, z, gn[0])


def _mla_mixer(z, params):
    batch = z.shape[0]
    return pl.pallas_call(
        _mla_kernel, grid=(batch,),
        in_specs=[_seq_spec(MLA_COLS)] + [_resident(prm) for prm in params],
        out_specs=_seq_spec(GROUP_WIDTH), out_shape=_y_shape(batch),
        scratch_shapes=[pltpu.VMEM((4, SEQ, LANES), BF16), pltpu.VMEM((4, SEQ, LANES), BF16),
                        pltpu.VMEM((4, MLA_VROWS, SEQ), BF16),
                        pltpu.VMEM((4, SEQ, MLA_QBLOCK), F32), pltpu.VMEM((4, SEQ, MLA_QBLOCK), F32),
                        pltpu.VMEM((4, 8, MLA_QBLOCK), F32), pltpu.VMEM((4, 8, MLA_QBLOCK), F32)],
        compiler_params=_cparams(), name="mla_mixer",
    )(z, *[arr for arr, _ in params])


def kernel(x, p, ffn1_norm, ffn1_w_gate, ffn1_w_up, ffn1_w_down, mix_norm, w_in, na_bias, swa_sink, mla_q_norm, mla_w_uq, mla_kv_norm, mla_w_ukv, group_norm, w_out, ffn2_norm, ffn2_w_gate, ffn2_w_up, ffn2_w_down, ple_norm, ple_w_gate, ple_w_proj, final_norm):
    batch, seq, d_model = x.shape
    assert (seq, d_model) == (SEQ, D_MODEL) and x.dtype == F32
    n_tok = batch * seq
    cos64, sin64 = _rope_tables_64()
    c_mla, s_mla = _rope_tables_mla()
    rows = lambda v: v.reshape(v.shape[:-1] + (1, v.shape[-1]))
    bf = lambda w: w.astype(BF16)
    ffn1 = [rows(ffn1_norm), bf(ffn1_w_gate), bf(ffn1_w_up), bf(ffn1_w_down), rows(mix_norm), _proj_weight(w_in)]
    ffn2 = [bf(w_out), rows(ffn2_norm), bf(ffn2_w_gate), bf(ffn2_w_up), bf(ffn2_w_down), rows(ple_norm),
            bf(ple_w_gate), bf(ple_w_proj)]
    mla = [rows(mla_q_norm), _wuq_weight(mla_w_uq), rows(mla_kv_norm), _wukv_weight(mla_w_ukv)]
    na_tables = _na_bias_tables(na_bias)
    gains = rows(group_norm)
    p3 = p.reshape(DEPTH, n_tok, PLE_DIM)

    x2 = x.reshape(n_tok, d_model)
    for i in range(DEPTH):
        at = lambda arrs: [(a, (i,)) for a in arrs]
        gain = lambda g: (gains, (i, g))
        x2, z_na, z_swa, z_dil, z_mla = _ffn1_proj(x2, at(ffn1), cos64, sin64)
        seq_view = lambda z: z.reshape(batch, seq, z.shape[-1])
        y_na = _na_mixer(seq_view(z_na), (na_tables, (i,)), gain(0))
        y_swa = _swa_mixer(i, seq_view(z_swa), swa_sink, gain(1))
        y_dil = _dil_mixer(seq_view(z_dil), gain(2))
        y_mla = _mla_mixer(seq_view(z_mla), at(mla) + [(c_mla, ()), (s_mla, ()), gain(3)])
        ys = [y.reshape(n_tok, GROUP_WIDTH) for y in (y_na, y_swa, y_dil, y_mla)]
        x2 = _out_ffn2_ple(i == DEPTH - 1, i, x2, ys, p3, at(ffn2) + [(final_norm.reshape(1, -1), ())])
    return x2.reshape(batch, seq, d_model)
```

```python
import functools
import math

import numpy as np
import jax
import jax.numpy as jnp
from jax import lax
from jax.experimental import pallas as pl
from jax.experimental.pallas import tpu as pltpu

F32 = jnp.float32
BF16 = jnp.bfloat16

LANES = 128
F32_SUBLANES = 8
BF16_SUBLANES = 16
VMEM_LIMIT_BYTES = 58 * 1024 * 1024

D_MODEL = 1024
SEQ = 2048
DEPTH = 4
HEAD_DIM = 64
N_GROUPS = 4
GROUP_WIDTH = 256
GROUP_HEADS = GROUP_WIDTH // HEAD_DIM
QKV_WIDTH = 3 * GROUP_WIDTH
D_FF = 2816
FF_CHUNK = 256
N_FF_CHUNKS = D_FF // FF_CHUNK
PLE_DIM = 256
ROPE_THETA = 10000.0
EPS = 1e-6
NEG_INF = -1e30
LOG2E = math.log2(math.e)

GRID_W = 64
NA_ROWS = 8
NA_COLS = 16
NA_KEYS = NA_ROWS * GRID_W
SWA_HALF = 128
SWA_BLOCK = 128
SWA_KEYS = SWA_BLOCK + 2 * SWA_HALF
DIL_PATTERNS = ((128, 1), (512, 4), (2048, 16))
DIL_STEPS = 64
DIL_BLOCK = 128
DIL_KEYS = DIL_BLOCK + 2 * DIL_STEPS
DIL_CLASSES = 4
MLA_Q_LORA = 256
MLA_KV_LORA = 128
MLA_NOPE = 64
MLA_ROPE = 32
MLA_QBLOCK = 256
MLA_VROWS = HEAD_DIM + BF16_SUBLANES
MLA_LATENT = MLA_Q_LORA + MLA_KV_LORA
MLA_COLS = MLA_LATENT + 2 * LANES

NA_OFF, SWA_OFF = 0, QKV_WIDTH
DIL_OFF = SWA_OFF + 2 * GROUP_WIDTH
MLA_OFF = DIL_OFF + QKV_WIDTH
P_NA, P_SWA, P_DIL, P_MLA = 0, QKV_WIDTH, 2 * QKV_WIDTH, 3 * QKV_WIDTH
P_COLS = P_MLA + MLA_COLS

TOKEN_TILE = 512
BAND_UNROLL = 4
NA_UNROLL = 8


def _cparams(n_axes=1):
    return pltpu.CompilerParams(dimension_semantics=("arbitrary",) * n_axes,
                                vmem_limit_bytes=VMEM_LIMIT_BYTES)


def _resident(param):
    arr, lead = param
    rest = arr.shape[len(lead):]
    return pl.BlockSpec((None,) * len(lead) + rest, lambda *_: lead + (0,) * len(rest),
                        pipeline_mode=pl.Buffered(1))


def _rms(x, g):
    return x * lax.rsqrt(jnp.mean(x * x, axis=-1, keepdims=True) + EPS) * g


def _aligned(start, multiple):
    return start if isinstance(start, int) else pl.multiple_of(start, multiple)


def _dot(a, b):
    return jnp.dot(a, b, preferred_element_type=F32)


def _dot_t(a, b):
    return lax.dot_general(a, b, (((1,), (1,)), ((), ())), preferred_element_type=F32)


def _swiglu_half_step(x, norm_ref, wg_ref, wu_ref, wd_ref, u_ref, acc_ref):
    u_ref[...] = _rms(x, norm_ref[...]).astype(BF16)
    acc_ref[...] = jnp.zeros_like(acc_ref)

    for j in range(N_FF_CHUNKS):
        cols = slice(j * FF_CHUNK, (j + 1) * FF_CHUNK)
        u = u_ref[...]
        g = _dot(u, wg_ref[:, cols])
        up = _dot(u, wu_ref[:, cols])
        a = (g * jax.nn.sigmoid(g) * up).astype(BF16)
        acc_ref[...] += _dot(a, wd_ref[cols, :])
    return x + 0.5 * acc_ref[...]


def _rope_tile(x, cos, sin_signed):
    lane = lax.broadcasted_iota(jnp.int32, x.shape, 1)
    first_half = (lane % HEAD_DIM) < (HEAD_DIM // 2)
    partner = jnp.where(first_half, pltpu.roll(x, LANES - HEAD_DIM // 2, 1), pltpu.roll(x, HEAD_DIM // 2, 1))
    return x * cos + partner * sin_signed


def _ffn1_proj_kernel(x_ref, n1_ref, wg_ref, wu_ref, wd_ref, mixn_ref, win_ref, cos_ref, sin_ref,
                      xo_ref, zna_ref, zswa_ref, zdil_ref, zmla_ref, u_ref, acc_ref):
    x = _swiglu_half_step(x_ref[...], n1_ref, wg_ref, wu_ref, wd_ref, u_ref, acc_ref)
    xo_ref[...] = x
    u_ref[...] = _rms(x, mixn_ref[...]).astype(BF16)
    cos = cos_ref[...]
    sin = sin_ref[...]
    q_scale = HEAD_DIM ** -0.5 * LOG2E

    def proj(col, width):
        return _dot(u_ref[...], win_ref[:, col:col + width])

    def roped(col, scale):
        z = proj(col, 2 * LANES)
        halves = [_rope_tile(z[:, h * LANES:(h + 1) * LANES], cos, sin) for h in range(2)]
        return jnp.concatenate(halves, axis=1) * scale

    q_cols, k_cols, v_cols = (slice(i * GROUP_WIDTH, (i + 1) * GROUP_WIDTH) for i in range(3))
    kv_cols = slice(GROUP_WIDTH, QKV_WIDTH)
    zna_ref[:, q_cols] = (proj(P_NA, GROUP_WIDTH) * q_scale).astype(BF16)
    zna_ref[:, kv_cols] = proj(P_NA + GROUP_WIDTH, 2 * GROUP_WIDTH).astype(BF16)
    zswa_ref[:, q_cols] = roped(P_SWA, q_scale).astype(BF16)
    zswa_ref[:, k_cols] = roped(P_SWA + GROUP_WIDTH, 1.0).astype(BF16)
    zswa_ref[:, v_cols] = proj(P_SWA + 2 * GROUP_WIDTH, GROUP_WIDTH).astype(BF16)
    zdil_ref[:, q_cols] = roped(P_DIL, q_scale)
    zdil_ref[:, k_cols] = roped(P_DIL + GROUP_WIDTH, 1.0)
    zdil_ref[:, v_cols] = proj(P_DIL + 2 * GROUP_WIDTH, GROUP_WIDTH)
    zmla_ref[:, 0:MLA_LATENT] = proj(P_MLA, MLA_LATENT)
    zmla_ref[:, MLA_LATENT:MLA_COLS] = proj(P_MLA + MLA_LATENT, 2 * LANES)


def _out_ffn2_ple_kernel(final, x_ref, yna_ref, yswa_ref, ydil_ref, ymla_ref, p_ref, wout_ref,
                         n2_ref, wg_ref, wu_ref, wd_ref, plen_ref, plewg_ref, plewp_ref, fin_ref,
                         xo_ref, u_ref, acc_ref):
    x = x_ref[...]
    mix = None
    for g, y_ref in enumerate((yna_ref, yswa_ref, ydil_ref, ymla_ref)):
        part = _dot(y_ref[...], wout_ref[g * GROUP_WIDTH:(g + 1) * GROUP_WIDTH, :])
        mix = part if mix is None else mix + part
    x = x + mix
    x = _swiglu_half_step(x, n2_ref, wg_ref, wu_ref, wd_ref, u_ref, acc_ref)
    gate = jax.nn.sigmoid(_dot(_rms(x, plen_ref[...]).astype(BF16), plewg_ref[...]))
    x = x + gate * _dot(p_ref[...].astype(BF16), plewp_ref[...])
    if final:
        x = _rms(x, fin_ref[...])
    xo_ref[...] = x


def _pair_attention(items):
    def stack_heads(q):
        lane = lax.broadcasted_iota(jnp.int32, q.shape, 1)
        zero = jnp.zeros_like(q)
        return jnp.concatenate([jnp.where(lane < HEAD_DIM, q, zero), jnp.where(lane >= HEAD_DIM, q, zero)], axis=0)

    def scores(st, it):
        q, k, _, bias2, _ = it
        st["s"] = _dot_t(stack_heads(q()), k()) + bias2()

    def row_max(st, it):
        m = jnp.max(st["s"], axis=-1, keepdims=True)
        st["m"] = m if it[4] is None else jnp.maximum(m, it[4])

    def probs(st, it):
        p = jnp.exp2(st.pop("s") - st["m"])
        l = jnp.sum(p, axis=-1, keepdims=True)
        st["l"] = l if it[4] is None else l + jnp.exp2(it[4] - st["m"])
        st["p"] = p.astype(BF16)

    def values(st, it):
        st["o"] = _dot(st.pop("p"), it[2]())

    stages = (scores, row_max, probs, values)
    states = [{} for _ in items]
    for wave in range(len(items) + len(stages) - 1):
        for s in reversed(range(len(stages))):
            i = wave - s
            if 0 <= i < len(items):
                stages[s](states[i], items[i])
    return [(st["m"], st["l"], st["o"]) for st in states]


def _merge_heads(x2, rows):
    top, bot = x2[:rows], x2[rows:]
    if top.shape[1] == 1:
        top = jnp.broadcast_to(top, (rows, LANES))
        bot = jnp.broadcast_to(bot, (rows, LANES))
    lane = lax.broadcasted_iota(jnp.int32, (rows, LANES), 1)
    return jnp.where(lane < HEAD_DIM, top, bot)


def _band_bias(offset, n_q, n_k, half_width):
    delta = offset + lax.broadcasted_iota(jnp.int32, (n_q, n_k), 1) - lax.broadcasted_iota(jnp.int32, (n_q, n_k), 0)
    bias = jnp.where(jnp.abs(delta) <= half_width, 0.0, NEG_INF).astype(F32)
    return jnp.concatenate([bias, bias], axis=0)


def _na_kernel(z_ref, bias_ref, gn_ref, o_ref):
    rows = SEQ // GRID_W

    def row_bias(pr, shift):
        return jnp.concatenate([bias_ref[pr, shift + 2 * t] for t in range(NA_ROWS // 2)], axis=1)

    def row_items(r):
        rs = jnp.clip(r - NA_ROWS // 2, 0, rows - NA_ROWS)
        shift = rs - r + NA_ROWS - 1
        q_rows = pl.ds(pl.multiple_of(r * GRID_W, GRID_W), GRID_W)
        k_rows = pl.ds(pl.multiple_of(rs * GRID_W, GRID_W), NA_KEYS)
        items = []
        for pr in range(2):
            tile = lambda rows_, base, pr=pr: z_ref[0, rows_, base + pr * LANES:base + (pr + 1) * LANES]
            items.append((functools.partial(tile, q_rows, 0), functools.partial(tile, k_rows, GROUP_WIDTH),
                          functools.partial(tile, k_rows, 2 * GROUP_WIDTH), functools.partial(row_bias, pr, shift),
                          None))
        return q_rows, items

    def body(it, carry):
        loaded = [row_items(it * NA_UNROLL + u) for u in range(NA_UNROLL)]
        stats = _pair_attention([item for _, items in loaded for item in items])
        outs = [_merge_heads(o / l, GRID_W) for _, l, o in stats]
        ys = [_rms(jnp.concatenate(outs[2 * u:2 * u + 2], axis=1), gn_ref[...]).astype(BF16)
              for u in range(NA_UNROLL)]
        for (q_rows, _), y in zip(loaded, ys):
            o_ref[0, q_rows, :] = y
        return carry

    lax.fori_loop(0, rows // NA_UNROLL, body, 0)


def _swa_kernel(layer, sink_ref, z_ref, gn_ref, o_ref):
    head_row = lax.broadcasted_iota(jnp.int32, (2 * SWA_BLOCK, 1), 0) < SWA_BLOCK
    sinks = [jnp.where(head_row, sink_ref[layer, 2 * g], sink_ref[layer, 2 * g + 1]).astype(F32) * LOG2E
             for g in range(2)]

    def block_items(i):
        s0 = pl.multiple_of(i * SWA_BLOCK, SWA_BLOCK)
        ks = pl.multiple_of(jnp.clip(s0 - SWA_HALF, 0, SEQ - SWA_KEYS), SWA_BLOCK)
        q_rows, k_rows = pl.ds(s0, SWA_BLOCK), pl.ds(ks, SWA_KEYS)
        bias2 = functools.cache(functools.partial(_band_bias, ks - s0, SWA_BLOCK, SWA_KEYS, SWA_HALF))
        items = []
        for g in range(2):
            tile = lambda rows_, base, g=g: z_ref[0, rows_, base + g * LANES:base + (g + 1) * LANES]
            items.append((functools.partial(tile, q_rows, 0), functools.partial(tile, k_rows, GROUP_WIDTH),
                          functools.partial(tile, k_rows, 2 * GROUP_WIDTH), bias2, sinks[g]))
        return q_rows, items

    def body(it, carry):
        loaded = [block_items(it * BAND_UNROLL + u) for u in range(BAND_UNROLL)]
        stats = _pair_attention([item for _, items in loaded for item in items])
        outs = [_merge_heads(o / l, SWA_BLOCK) for _, l, o in stats]
        ys = [_rms(jnp.concatenate(outs[2 * u:2 * u + 2], axis=1), gn_ref[...]).astype(BF16)
              for u in range(BAND_UNROLL)]
        for (q_rows, _), y in zip(loaded, ys):
            o_ref[0, q_rows, :] = y
        return carry

    lax.fori_loop(0, SEQ // SWA_BLOCK // BAND_UNROLL, body, 0)


def _dil_kernel(q0_ref, q1_ref, k0_ref, k1_ref, v0_ref, v1_ref, gn_ref, o_ref,
                cls_scr, s1_scr, s3_scr, y_scr):
    in_refs = (q0_ref, q1_ref, k0_ref, k1_ref, v0_ref, v1_ref)
    cls_len = SEQ // DIL_CLASSES

    for arr, ref in enumerate(in_refs):
        for a in range(DIL_CLASSES):
            cls_scr[arr, a * cls_len:(a + 1) * cls_len, :] = ref[0, pl.ds(a, cls_len, stride=DIL_CLASSES), :]

    def run(n_units, unroll, unit_items, emit):
        def body(it, carry):
            units = [unit_items(it * unroll + j) for j in range(unroll)]
            stats = _pair_attention([item for _, items in units for item in items])
            merged = [tuple(_merge_heads(val, DIL_BLOCK) for val in new) for new in stats]
            for j, (info, _) in enumerate(units):
                emit(info, merged[2 * j:2 * j + 2])
            return carry
        lax.fori_loop(0, n_units // unroll, body, 0)

    def store_stats(out_scr):
        def emit(rows, pair_stats):
            for pr, new in enumerate(pair_stats):
                for stat, val in enumerate(new):
                    out_scr[stat, pr, rows, :] = val
        return emit

    def band(k_lo, q_lo, n_keys):
        return functools.cache(functools.partial(_band_bias, k_lo - q_lo, DIL_BLOCK, n_keys, DIL_STEPS))

    def p1_unit(u):
        q_lo = pl.multiple_of(u * DIL_BLOCK, DIL_BLOCK)
        k_lo = pl.multiple_of(jnp.clip(q_lo - DIL_STEPS, 0, SEQ - DIL_KEYS), DIL_STEPS)
        q_rows, k_rows = pl.ds(q_lo, DIL_BLOCK), pl.ds(k_lo, DIL_KEYS)
        bias2 = band(k_lo, q_lo, DIL_KEYS)
        tile = lambda ref, rows: ref[0, rows, :].astype(BF16)
        return q_rows, [(functools.partial(tile, in_refs[pr], q_rows), functools.partial(tile, in_refs[2 + pr], k_rows),
                         functools.partial(tile, in_refs[4 + pr], k_rows), bias2, None) for pr in range(2)]

    run(SEQ // DIL_BLOCK, NA_UNROLL, p1_unit, store_stats(s1_scr))

    cls_tile = lambda arr, rows: cls_scr[arr, rows, :].astype(BF16)

    sub = DIL_PATTERNS[2][1] // DIL_CLASSES
    bias3 = functools.cache(functools.partial(_band_bias, 0, DIL_BLOCK, DIL_BLOCK, DIL_STEPS))

    def p3_unit(u):
        rows = pl.ds((u // sub) * cls_len + (u % sub), DIL_BLOCK, stride=sub)
        return rows, [(functools.partial(cls_tile, pr, rows), functools.partial(cls_tile, 2 + pr, rows),
                       functools.partial(cls_tile, 4 + pr, rows), bias3, None) for pr in range(2)]

    run(DIL_PATTERNS[2][1], BAND_UNROLL, p3_unit, store_stats(s3_scr))

    blocks_per_cls = cls_len // DIL_BLOCK

    def p2_unit(u):
        base = (u // blocks_per_cls) * cls_len
        q_lo = (u % blocks_per_cls) * DIL_BLOCK
        k_lo = jnp.clip(q_lo - DIL_STEPS, 0, cls_len - DIL_KEYS)
        q_rows = pl.ds(pl.multiple_of(base + q_lo, DIL_BLOCK), DIL_BLOCK)
        k_rows = pl.ds(pl.multiple_of(base + k_lo, DIL_STEPS), DIL_KEYS)
        seq_rows = pl.ds(u // blocks_per_cls + DIL_CLASSES * q_lo, DIL_BLOCK, stride=DIL_CLASSES)
        bias2 = band(k_lo, q_lo, DIL_KEYS)
        return (q_rows, seq_rows), [(functools.partial(cls_tile, pr, q_rows), functools.partial(cls_tile, 2 + pr, k_rows),
                                     functools.partial(cls_tile, 4 + pr, k_rows), bias2, None) for pr in range(2)]

    def merge_and_write(info, pair_stats):
        cls_rows, seq_rows = info
        ys = []
        for pr, (m2, l2, o2) in enumerate(pair_stats):
            m1, m3 = s1_scr[0, pr, seq_rows, :], s3_scr[0, pr, cls_rows, :]
            m_all = jnp.maximum(jnp.maximum(m1, m2), m3)
            w1, w2, w3 = jnp.exp2(m1 - m_all), jnp.exp2(m2 - m_all), jnp.exp2(m3 - m_all)
            den = s1_scr[1, pr, seq_rows, :] * w1 + l2 * w2 + s3_scr[1, pr, cls_rows, :] * w3
            num = s1_scr[2, pr, seq_rows, :] * w1 + o2 * w2 + s3_scr[2, pr, cls_rows, :] * w3
            ys.append(num / den)
        y = _rms(jnp.concatenate(ys, axis=1), gn_ref[...])
        for pr in range(2):
            y_scr[pr, seq_rows, :] = y[:, pr * LANES:(pr + 1) * LANES]

    run(DIL_CLASSES * blocks_per_cls, BAND_UNROLL, p2_unit, merge_and_write)
    o_ref[0] = jnp.concatenate([y_scr[0], y_scr[1]], axis=1).astype(BF16)


def _mla_kernel(z_ref, qn_ref, wuq_ref, kvn_ref, wukv_ref, c_ref, s_ref, gn_ref, o_ref,
                q_scr, k_scr, vt_scr, sa_scr, sb_scr, ma_scr, mb_scr):
    scale = (MLA_NOPE + MLA_ROPE) ** -0.5 * LOG2E
    c_tab = c_ref[...]
    s_tab = s_ref[...]
    heads = range(GROUP_HEADS)
    cqn = _rms(z_ref[0, :, 0:MLA_Q_LORA], qn_ref[...]).astype(BF16)
    for h in heads:
        g = _dot(cqn, wuq_ref[:, 2 * h * LANES:2 * (h + 1) * LANES])
        q_scr[h] = ((g[:, :LANES] * c_tab + g[:, LANES:] * s_tab) * scale).astype(BF16)
    ckvn = _rms(z_ref[0, :, MLA_Q_LORA:MLA_LATENT], kvn_ref[...]).astype(BF16)
    k_pe = z_ref[0, :, MLA_LATENT:MLA_LATENT + LANES] * c_tab + z_ref[0, :, MLA_LATENT + LANES:MLA_COLS] * s_tab
    for h in heads:
        k_scr[h] = (_dot(ckvn, wukv_ref[:, h * LANES:(h + 1) * LANES]) + k_pe).astype(BF16)
    vt = _dot(ckvn, wukv_ref[:, GROUP_HEADS * LANES:GROUP_HEADS * LANES + GROUP_WIDTH]).T
    pad_row = lax.broadcasted_iota(jnp.int32, (MLA_VROWS - HEAD_DIM, SEQ), 0)
    ones_then_zeros = jnp.where(pad_row == 0, 1.0, 0.0).astype(F32)
    for h in heads:
        vt_scr[h] = jnp.concatenate([vt[h * HEAD_DIM:(h + 1) * HEAD_DIM], ones_then_zeros], axis=0).astype(BF16)

    n_blocks = SEQ // MLA_QBLOCK

    def scores(h, blk, s_scr, m_scr):
        q0 = _aligned(blk * MLA_QBLOCK, MLA_QBLOCK)
        s = _dot_t(k_scr[h], q_scr[h, pl.ds(q0, MLA_QBLOCK), :])
        s_scr[h] = s
        m_scr[h] = jnp.broadcast_to(jnp.max(s, axis=0, keepdims=True), (F32_SUBLANES, MLA_QBLOCK))

    def finish(cur, blk, nxt, nxt_blk):
        yts = []
        for h in heads:
            if nxt_blk is not None:
                scores(h, nxt_blk, *nxt)
            s_scr, m_scr = cur
            p = jnp.exp2(s_scr[h] - m_scr[h][0:1, :])
            ol = _dot(vt_scr[h], p.astype(BF16))
            yts.append(ol[:HEAD_DIM] / ol[HEAD_DIM:HEAD_DIM + 1])
        y = jnp.concatenate(yts, axis=0).T
        q0 = _aligned(blk * MLA_QBLOCK, MLA_QBLOCK)
        o_ref[0, pl.ds(q0, MLA_QBLOCK), :] = _rms(y, gn_ref[...]).astype(BF16)

    buf_a, buf_b = (sa_scr, ma_scr), (sb_scr, mb_scr)
    for h in heads:
        scores(h, 0, *buf_a)

    def two_blocks(j, carry):
        finish(buf_a, 2 * j, buf_b, 2 * j + 1)
        finish(buf_b, 2 * j + 1, buf_a, 2 * j + 2)
        return carry

    lax.fori_loop(0, n_blocks // 2 - 1, two_blocks, 0)
    finish(buf_a, n_blocks - 2, buf_b, n_blocks - 1)
    finish(buf_b, n_blocks - 1, buf_a, None)


def _rope_tables_64():
    half = HEAD_DIM // 2
    inv_freq = np.float32(ROPE_THETA) ** (-np.arange(half, dtype=np.float32) / np.float32(half))
    ang = np.arange(SEQ, dtype=np.float32)[:, None] * inv_freq[None, :]
    cos, sin = np.cos(ang).astype(np.float32), np.sin(ang).astype(np.float32)
    cos_h = np.concatenate([cos, cos], axis=1)
    sin_h = np.concatenate([-sin, sin], axis=1)
    return jnp.asarray(np.tile(cos_h, (1, 2))), jnp.asarray(np.tile(sin_h, (1, 2)))


def _rope_tables_mla():
    half = MLA_ROPE // 2
    inv_freq = np.float32(ROPE_THETA) ** (-np.arange(half, dtype=np.float32) / np.float32(half))
    ang = np.arange(SEQ, dtype=np.float32)[:, None] * inv_freq[None, :]
    cos, sin = np.cos(ang).astype(np.float32), np.sin(ang).astype(np.float32)
    ones = np.ones((SEQ, MLA_NOPE), np.float32)
    zeros_n = np.zeros((SEQ, MLA_NOPE), np.float32)
    zeros_p = np.zeros((SEQ, LANES - MLA_NOPE - MLA_ROPE), np.float32)
    c_tab = np.concatenate([ones, cos, cos, zeros_p], axis=1)
    s_tab = np.concatenate([zeros_n, -sin, sin, zeros_p], axis=1)
    return jnp.asarray(c_tab), jnp.asarray(s_tab)


def _cols(w, pieces):
    parts = [jnp.zeros(w.shape[:-1] + (pc,), w.dtype) if isinstance(pc, int) else w[..., pc[0]:pc[0] + pc[1]]
             for pc in pieces]
    return jnp.concatenate(parts, axis=-1)


def _swapped(start, width):
    return [(start + width // 2, width // 2), (start, width // 2)]


def _proj_weight(w_in):
    pad = LANES - MLA_NOPE - MLA_ROPE
    kpe = MLA_OFF + MLA_LATENT
    k_off = SWA_OFF + GROUP_WIDTH
    k0, k1, v0, v1 = ((k_off + i * HEAD_DIM, HEAD_DIM) for i in range(4))
    pieces = ([(NA_OFF, QKV_WIDTH), (SWA_OFF, GROUP_WIDTH), k0, k0, k1, k1, v0, v0, v1, v1, (DIL_OFF, QKV_WIDTH),
               (MLA_OFF, MLA_LATENT), MLA_NOPE, (kpe, MLA_ROPE), pad, MLA_NOPE]
              + _swapped(kpe, MLA_ROPE) + [pad])
    w = _cols(w_in.astype(BF16), pieces)
    assert w.shape[-1] == P_COLS
    return w


def _wuq_weight(w_uq):
    pad = LANES - MLA_NOPE - MLA_ROPE
    pieces = []
    for h in range(GROUP_HEADS):
        base = h * (MLA_NOPE + MLA_ROPE)
        pieces += [(base, MLA_NOPE + MLA_ROPE), pad, MLA_NOPE] + _swapped(base + MLA_NOPE, MLA_ROPE) + [pad]
    return _cols(w_uq.astype(BF16), pieces)


def _wukv_weight(w_ukv):
    per_head = MLA_NOPE + HEAD_DIM
    pieces = []
    for h in range(GROUP_HEADS):
        pieces += [(h * per_head, MLA_NOPE), LANES - MLA_NOPE]
    pieces += [(h * per_head + MLA_NOPE, HEAD_DIM) for h in range(GROUP_HEADS)]
    return _cols(w_ukv.astype(BF16), pieces)


def _na_bias_tables(bias):
    n_layers = bias.shape[0]
    qc = np.arange(GRID_W)[:, None]
    kc = np.arange(GRID_W)[None, :]
    ws = np.clip(qc - NA_COLS // 2, 0, GRID_W - NA_COLS)
    valid = jnp.asarray((kc >= ws) & (kc < ws + NA_COLS))
    lo = GRID_W - NA_COLS
    span = 2 * GRID_W - 1
    padded = jnp.pad(bias, ((0, 0), (0, 0), (0, 0), (lo, lo + 1)))
    skew = jnp.broadcast_to(padded[..., None, :], padded.shape[:-1] + (GRID_W, span + 1))
    skew = skew.reshape(padded.shape[:-1] + (GRID_W * (span + 1),))[..., :GRID_W * span]
    toep = skew.reshape(padded.shape[:-1] + (GRID_W, span))[..., GRID_W - 1:]
    toep = jnp.where(valid, toep * LOG2E, NEG_INF)
    two_rows = jnp.concatenate([toep[:, :, :-1], toep[:, :, 1:]], axis=-1)
    n_off = two_rows.shape[2]
    g = two_rows.reshape(n_layers, 2, 2, n_off, GRID_W, LANES).transpose(0, 1, 3, 2, 4, 5)
    return g.reshape(n_layers, 2, n_off, 2 * GRID_W, LANES)


def _tok_spec(width):
    return pl.BlockSpec((TOKEN_TILE, width), lambda i: (i, 0))


def _ffn1_proj(x2, params, cos, sin):
    n_tok = x2.shape[0]
    tiles_per_seq = SEQ // TOKEN_TILE
    pos_spec = pl.BlockSpec((TOKEN_TILE, LANES), lambda i: (i % tiles_per_seq, 0))
    return pl.pallas_call(
        _ffn1_proj_kernel,
        grid=(n_tok // TOKEN_TILE,),
        in_specs=[_tok_spec(D_MODEL)] + [_resident(prm) for prm in params] + [pos_spec, pos_spec],
        out_specs=[_tok_spec(D_MODEL)] + [_tok_spec(QKV_WIDTH)] * 3 + [_tok_spec(MLA_COLS)],
        out_shape=[jax.ShapeDtypeStruct((n_tok, D_MODEL), F32),
                   jax.ShapeDtypeStruct((n_tok, QKV_WIDTH), BF16),
                   jax.ShapeDtypeStruct((n_tok, QKV_WIDTH), BF16),
                   jax.ShapeDtypeStruct((n_tok, QKV_WIDTH), F32),
                   jax.ShapeDtypeStruct((n_tok, MLA_COLS), F32)],
        scratch_shapes=[pltpu.VMEM((TOKEN_TILE, D_MODEL), BF16), pltpu.VMEM((TOKEN_TILE, D_MODEL), F32)],
        compiler_params=_cparams(),
        name="ffn1_proj",
    )(x2, *[arr for arr, _ in params], cos, sin)


def _out_ffn2_ple(final, layer, x2, ys, p3, params):
    n_tok = x2.shape[0]
    p_spec = pl.BlockSpec((None, TOKEN_TILE, PLE_DIM), lambda i: (layer, i, 0))
    return pl.pallas_call(
        functools.partial(_out_ffn2_ple_kernel, final),
        grid=(n_tok // TOKEN_TILE,),
        in_specs=[_tok_spec(D_MODEL)] + [_tok_spec(GROUP_WIDTH)] * 4 + [p_spec]
        + [_resident(prm) for prm in params],
        out_specs=_tok_spec(D_MODEL),
        out_shape=jax.ShapeDtypeStruct((n_tok, D_MODEL), F32),
        scratch_shapes=[pltpu.VMEM((TOKEN_TILE, D_MODEL), BF16), pltpu.VMEM((TOKEN_TILE, D_MODEL), F32)],
        compiler_params=_cparams(),
        name="out_ffn2_ple",
    )(x2, *ys, p3, *[arr for arr, _ in params])


def _seq_spec(width):
    return pl.BlockSpec((1, SEQ, width), lambda b: (b, 0, 0))


def _y_shape(batch):
    return jax.ShapeDtypeStruct((batch, SEQ, GROUP_WIDTH), BF16)


def _na_mixer(z, bias_tab, gn):
    batch = z.shape[0]
    return pl.pallas_call(
        _na_kernel, grid=(batch,),
        in_specs=[_seq_spec(QKV_WIDTH), _resident(bias_tab), _resident(gn)],
        out_specs=_seq_spec(GROUP_WIDTH), out_shape=_y_shape(batch),
        compiler_params=_cparams(), name="na_mixer",
    )(z, bias_tab[0], gn[0])


def _swa_mixer(layer, z, sinks, gn):
    batch = z.shape[0]
    return pl.pallas_call(
        functools.partial(_swa_kernel, layer), grid=(batch,),
        in_specs=[pl.BlockSpec(memory_space=pltpu.SMEM), _seq_spec(QKV_WIDTH), _resident(gn)],
        out_specs=_seq_spec(GROUP_WIDTH), out_shape=_y_shape(batch),
        compiler_params=_cparams(), name="swa_mixer",
    )(sinks, z, gn[0])


def _dil_mixer(z, gn):
    batch = z.shape[0]
    n_slabs = QKV_WIDTH // LANES
    pair_specs = [pl.BlockSpec((1, SEQ, LANES), functools.partial(lambda c, b: (b, 0, c), c)) for c in range(n_slabs)]
    stats = pltpu.VMEM((3, 2, SEQ, LANES), F32)
    return pl.pallas_call(
        _dil_kernel, grid=(batch,),
        in_specs=pair_specs + [_resident(gn)],
        out_specs=_seq_spec(GROUP_WIDTH), out_shape=_y_shape(batch),
        scratch_shapes=[pltpu.VMEM((n_slabs, SEQ, LANES), F32), stats, stats, pltpu.VMEM((2, SEQ, LANES), F32)],
        compiler_params=_cparams(), name="dil_mixer",
    )(*([z] * n_slabs), gn[0])


def _mla_mixer(z, params):
    batch = z.shape[0]
    return pl.pallas_call(
        _mla_kernel, grid=(batch,),
        in_specs=[_seq_spec(MLA_COLS)] + [_resident(prm) for prm in params],
        out_specs=_seq_spec(GROUP_WIDTH), out_shape=_y_shape(batch),
        scratch_shapes=[pltpu.VMEM((GROUP_HEADS, SEQ, LANES), BF16), pltpu.VMEM((GROUP_HEADS, SEQ, LANES), BF16),
                        pltpu.VMEM((GROUP_HEADS, MLA_VROWS, SEQ), BF16)]
        + [pltpu.VMEM((GROUP_HEADS, SEQ, MLA_QBLOCK), F32)] * 2
        + [pltpu.VMEM((GROUP_HEADS, F32_SUBLANES, MLA_QBLOCK), F32)] * 2,
        compiler_params=_cparams(), name="mla_mixer",
    )(z, *[arr for arr, _ in params])


def kernel(x, p, ffn1_norm, ffn1_w_gate, ffn1_w_up, ffn1_w_down, mix_norm, w_in, na_bias, swa_sink, mla_q_norm, mla_w_uq, mla_kv_norm, mla_w_ukv, group_norm, w_out, ffn2_norm, ffn2_w_gate, ffn2_w_up, ffn2_w_down, ple_norm, ple_w_gate, ple_w_proj, final_norm):
    batch, seq, d_model = x.shape
    assert (seq, d_model) == (SEQ, D_MODEL) and x.dtype == F32
    n_tok = batch * seq
    cos64, sin64 = _rope_tables_64()
    c_mla, s_mla = _rope_tables_mla()
    rows = lambda v: v.reshape(v.shape[:-1] + (1, v.shape[-1]))
    bf = lambda w: w.astype(BF16)
    ffn1 = [rows(ffn1_norm), bf(ffn1_w_gate), bf(ffn1_w_up), bf(ffn1_w_down), rows(mix_norm), _proj_weight(w_in)]
    ffn2 = [bf(w_out), rows(ffn2_norm), bf(ffn2_w_gate), bf(ffn2_w_up), bf(ffn2_w_down), rows(ple_norm),
            bf(ple_w_gate), bf(ple_w_proj)]
    mla = [rows(mla_q_norm), _wuq_weight(mla_w_uq), rows(mla_kv_norm), _wukv_weight(mla_w_ukv)]
    na_tables = _na_bias_tables(na_bias)
    gains = rows(group_norm)
    p3 = p.reshape(DEPTH, n_tok, PLE_DIM)

    x2 = x.reshape(n_tok, d_model)
    for i in range(DEPTH):
        at = lambda arrs: [(a, (i,)) for a in arrs]
        gain = lambda g: (gains, (i, g))
        x2, z_na, z_swa, z_dil, z_mla = _ffn1_proj(x2, at(ffn1), cos64, sin64)
        seq_view = lambda z: z.reshape(batch, seq, z.shape[-1])
        y_na = _na_mixer(seq_view(z_na), (na_tables, (i,)), gain(0))
        y_swa = _swa_mixer(i, seq_view(z_swa), swa_sink, gain(1))
        y_dil = _dil_mixer(seq_view(z_dil), gain(2))
        y_mla = _mla_mixer(seq_view(z_mla), at(mla) + [(c_mla, ()), (s_mla, ()), gain(3)])
        ys = [y.reshape(n_tok, GROUP_WIDTH) for y in (y_na, y_swa, y_dil, y_mla)]
        x2 = _out_ffn2_ple(i == DEPTH - 1, i, x2, ys, p3, at(ffn2) + [(final_norm.reshape(1, -1), ())])
    return x2.reshape(batch, seq, d_model)
```
